```python
import jax, jax.numpy as jnp
from jax import lax
import numpy as np

D_MODEL = 4096
BATCH = 4
SEQ = 2048
DEPTH = 4
DEC_BATCH = 8
DEC_SEQ = 1
PAST_LEN = 8192
PAGE_SIZE = 128

HEAD_DIM = 128
SB_HEADS = D_MODEL // (2 * HEAD_DIM)
RET_HEADS = D_MODEL // (2 * HEAD_DIM)
SB_WIDTH = SB_HEADS * HEAD_DIM
RET_WIDTH = RET_HEADS * HEAD_DIM
MIX_WIDTH = SB_WIDTH + RET_WIDTH
IN_COLS = 3 * SB_WIDTH + 4 * RET_WIDTH
SPLITS = [SB_WIDTH, 2 * SB_WIDTH, 3 * SB_WIDTH,
          3 * SB_WIDTH + RET_WIDTH, 3 * SB_WIDTH + 2 * RET_WIDTH, 3 * SB_WIDTH + 3 * RET_WIDTH]
D_FF = 11008
Q_BLOCK = 128
RET_CHUNK = 128
ROPE_BASE = 10000.0
NORM_EPS = 1e-6
GN_EPS = 1e-5
SB_BIAS_INIT = -7.0

kernel_name = "hybrid_stickbreaking_retention_decoder_step"

F32 = jnp.float32


def rms_norm(x, g):
    xf = x.astype(F32)
    y = xf * lax.rsqrt(jnp.mean(xf * xf, axis=-1, keepdims=True) + NORM_EPS)
    return (y * g.astype(F32)).astype(x.dtype)


def head_group_norm(o, g):
    of = o.astype(F32)
    mu = jnp.mean(of, axis=-1, keepdims=True)
    var = jnp.mean(jnp.square(of - mu), axis=-1, keepdims=True)
    return ((of - mu) * lax.rsqrt(var + GN_EPS) * g.astype(F32)).astype(o.dtype)


def half_ffn(x, g_pre, g_post, w_gate, w_up, w_down):
    h = rms_norm(x, g_pre)
    f = (jax.nn.silu(h @ w_gate) * (h @ w_up)) @ w_down
    return x + 0.5 * rms_norm(f, g_post)


def rope(x, pos):
    half = HEAD_DIM // 2
    inv = ROPE_BASE ** (-jnp.arange(half, dtype=F32) / half)
    ang = pos.astype(F32)[:, None] * inv[None, :]
    cos = jnp.cos(ang)[None, :, None, :]
    sin = jnp.sin(ang)[None, :, None, :]
    xf = x.astype(F32)
    x1, x2 = xf[..., :half], xf[..., half:]
    return jnp.concatenate([x1 * cos - x2 * sin, x1 * sin + x2 * cos], axis=-1).astype(x.dtype)


def project(h, w_in, pos):
    B, T, _ = h.shape
    q_sb, k_sb, v_sb, q_r, k_r, v_r, g_r = jnp.split(h @ w_in, SPLITS, axis=-1)
    heads = lambda t: t.reshape(B, T, -1, HEAD_DIM)
    q_r = rope(heads(q_r), pos)
    k_r = rope(heads(k_r), pos) * (HEAD_DIM ** -0.5)
    return heads(q_sb), heads(k_sb), heads(v_sb), q_r, k_r, heads(v_r), g_r


def sb_attend(q, k, v, bias, q_pos, k_pos):
    z = jnp.einsum("bqhd,bkhd->bhqk", q, k, preferred_element_type=F32) * (HEAD_DIM ** -0.5)
    z = z + bias.astype(F32)[None, :, None, None]
    valid = (k_pos[None, :] < q_pos[:, None])[None, None]
    log_1m = jnp.where(valid, jax.nn.log_sigmoid(-z), 0.0)
    later = lax.cumsum(log_1m, axis=3, reverse=True) - log_1m
    w = jnp.where(valid, jnp.exp(jax.nn.log_sigmoid(z) + later), 0.0)
    o = jnp.einsum("bhqk,bkhd->bqhd", w.astype(v.dtype), v, preferred_element_type=F32)
    return o.astype(v.dtype)


def sb_prompt(q, k, v, bias):
    B, T = q.shape[:2]
    n_blocks = T // Q_BLOCK
    qb = q.reshape(B, n_blocks, Q_BLOCK, SB_HEADS, HEAD_DIM).swapaxes(0, 1)
    k_pos = jnp.arange(T)

    def one_block(args):
        qi, i = args
        q_pos = i * Q_BLOCK + jnp.arange(Q_BLOCK)
        return sb_attend(qi, k, v, bias, q_pos, k_pos)

    out = lax.map(one_block, (qb, jnp.arange(n_blocks)))
    return out.swapaxes(0, 1).reshape(B, T, SB_HEADS, HEAD_DIM)


def log_gammas():
    return jnp.log1p(-(2.0 ** (-5.0 - jnp.arange(RET_HEADS, dtype=F32))))


def retention_chunk(state, q, k, v, log_g):
    L = q.shape[1]
    idx = jnp.arange(L, dtype=F32)
    diff = idx[:, None] - idx[None, :]
    decay = jnp.where(diff >= 0, jnp.exp(log_g[:, None, None] * jnp.maximum(diff, 0.0)), 0.0)
    qf, kf, vf = q.astype(F32), k.astype(F32), v.astype(F32)
    scores = jnp.einsum("blhd,bmhd->bhlm", qf, kf) * decay[None]
    o = jnp.einsum("bhlm,bmhe->blhe", scores, vf)
    q_dec = jnp.exp(log_g[None, :] * (idx[:, None] + 1.0))
    o = o + jnp.einsum("blhd,bhde->blhe", qf, state) * q_dec[None, :, :, None]
    k_dec = jnp.exp(log_g[None, :] * (L - 1.0 - idx[:, None]))
    new_state = state * jnp.exp(log_g * L)[None, :, None, None] + \
        jnp.einsum("blhd,blhe->bhde", kf * k_dec[None, :, :, None], vf)
    return new_state, o


def ret_prompt(q, k, v, log_g):
    B, T = q.shape[:2]
    n = T // RET_CHUNK
    to_chunks = lambda t: t.reshape(B, n, RET_CHUNK, RET_HEADS, HEAD_DIM).swapaxes(0, 1)
    state0 = jnp.zeros((B, RET_HEADS, HEAD_DIM, HEAD_DIM), F32)
    step = lambda s, c: retention_chunk(s, c[0], c[1], c[2], log_g)
    st, o = lax.scan(step, state0, (to_chunks(q), to_chunks(k), to_chunks(v)))
    return st, o.swapaxes(0, 1).reshape(B, T, RET_HEADS, HEAD_DIM).astype(q.dtype)


def merge(o_sb, o_r, g_r, sb_gain, gn_gain, w_out):
    B, T = o_sb.shape[:2]
    a = rms_norm(o_sb, sb_gain).reshape(B, T, SB_WIDTH)
    r = jax.nn.silu(g_r) * head_group_norm(o_r, gn_gain).reshape(B, T, RET_WIDTH)
    return jnp.concatenate([a, r], axis=-1) @ w_out


def setup_inputs(seed: int = 0) -> dict:
    key = jax.random.key(seed)
    ks = jax.random.split(key, 24)
    n_pages = PAST_LEN // PAGE_SIZE
    n_used = DEC_BATCH * n_pages
    n_pool = n_used + (n_used + 3) // 4

    def w(k, shape, fan_in):
        return jax.random.normal(k, shape, F32) * (fan_in ** -0.5)

    def gain(k, shape):
        return 1.0 + 0.05 * jax.random.normal(k, shape, F32)

    page_table = jax.random.permutation(ks[5], n_pool)[:n_used].reshape(DEC_BATCH, n_pages).astype(jnp.int32)
    return {
        "x_prompt": jax.random.normal(ks[0], (BATCH, SEQ, D_MODEL), F32),
        "x_sample": jax.random.normal(ks[1], (DEC_BATCH, DEC_SEQ, D_MODEL), F32),
        "cache_k": jax.random.normal(ks[2], (DEPTH, n_pool, PAGE_SIZE, SB_HEADS, HEAD_DIM), F32),
        "cache_v": jax.random.normal(ks[3], (DEPTH, n_pool, PAGE_SIZE, SB_HEADS, HEAD_DIM), F32),
        "state_ret": 0.1 * jax.random.normal(ks[4], (DEPTH, DEC_BATCH, RET_HEADS, HEAD_DIM, HEAD_DIM), F32),
        "page_table": page_table,
        "w_in": w(ks[6], (DEPTH, D_MODEL, IN_COLS), D_MODEL),
        "w_out": w(ks[7], (DEPTH, MIX_WIDTH, D_MODEL), MIX_WIDTH),
        "sb_gain": gain(ks[8], (DEPTH, SB_HEADS, HEAD_DIM)),
        "gn_gain": gain(ks[9], (DEPTH, RET_HEADS, HEAD_DIM)),
        "sb_bias": SB_BIAS_INIT + 0.5 * jax.random.normal(ks[22], (DEPTH, SB_HEADS), F32),
        "ffn1_gate": w(ks[10], (DEPTH, D_MODEL, D_FF), D_MODEL),
        "ffn1_up": w(ks[11], (DEPTH, D_MODEL, D_FF), D_MODEL),
        "ffn1_down": w(ks[12], (DEPTH, D_FF, D_MODEL), D_FF),
        "ffn2_gate": w(ks[13], (DEPTH, D_MODEL, D_FF), D_MODEL),
        "ffn2_up": w(ks[14], (DEPTH, D_MODEL, D_FF), D_MODEL),
        "ffn2_down": w(ks[15], (DEPTH, D_FF, D_MODEL), D_FF),
        "norm_ffn1_pre": gain(ks[16], (DEPTH, D_MODEL)),
        "norm_ffn1_post": gain(ks[17], (DEPTH, D_MODEL)),
        "norm_mix_pre": gain(ks[18], (DEPTH, D_MODEL)),
        "norm_mix_post": gain(ks[19], (DEPTH, D_MODEL)),
        "norm_ffn2_pre": gain(ks[20], (DEPTH, D_MODEL)),
        "norm_ffn2_post": gain(ks[21], (DEPTH, D_MODEL)),
    }


def reference(x_prompt, x_sample, cache_k, cache_v, state_ret, page_table,
              w_in, w_out, sb_gain, gn_gain, sb_bias,
              ffn1_gate, ffn1_up, ffn1_down, ffn2_gate, ffn2_up, ffn2_down,
              norm_ffn1_pre, norm_ffn1_post, norm_mix_pre, norm_mix_post,
              norm_ffn2_pre, norm_ffn2_post):
    log_g = log_gammas()
    pos_p = jnp.arange(SEQ)
    pos_s = PAST_LEN + jnp.arange(DEC_SEQ)
    k_pos_s = jnp.arange(PAST_LEN + DEC_SEQ)
    xp, xs = x_prompt, x_sample
    kp_l, vp_l, stp_l, ks_l, vs_l, sts_l = [], [], [], [], [], []
    for l in range(DEPTH):
        xp = half_ffn(xp, norm_ffn1_pre[l], norm_ffn1_post[l], ffn1_gate[l], ffn1_up[l], ffn1_down[l])
        xs = half_ffn(xs, norm_ffn1_pre[l], norm_ffn1_post[l], ffn1_gate[l], ffn1_up[l], ffn1_down[l])

        hp = rms_norm(xp, norm_mix_pre[l])
        q_sb, k_sb, v_sb, q_r, k_r, v_r, g_r = project(hp, w_in[l], pos_p)
        o_sb = sb_prompt(q_sb, k_sb, v_sb, sb_bias[l])
        st_p, o_r = ret_prompt(q_r, k_r, v_r, log_g)
        xp = xp + rms_norm(merge(o_sb, o_r, g_r, sb_gain[l], gn_gain[l], w_out[l]), norm_mix_post[l])
        kp_l.append(k_sb)
        vp_l.append(v_sb)
        stp_l.append(st_p.astype(state_ret.dtype))

        hs = rms_norm(xs, norm_mix_pre[l])
        sq_sb, sk_sb, sv_sb, sq_r, sk_r, sv_r, sg_r = project(hs, w_in[l], pos_s)
        past_k = cache_k[l][page_table].reshape(DEC_BATCH, PAST_LEN, SB_HEADS, HEAD_DIM)
        past_v = cache_v[l][page_table].reshape(DEC_BATCH, PAST_LEN, SB_HEADS, HEAD_DIM)
        k_all = jnp.concatenate([past_k.astype(sk_sb.dtype), sk_sb], axis=1)
        v_all = jnp.concatenate([past_v.astype(sv_sb.dtype), sv_sb], axis=1)
        so_sb = sb_attend(sq_sb, k_all, v_all, sb_bias[l], pos_s, k_pos_s)
        st_s, so_r = retention_chunk(state_ret[l].astype(F32), sq_r, sk_r, sv_r, log_g)
        xs = xs + rms_norm(merge(so_sb, so_r.astype(xs.dtype), sg_r, sb_gain[l], gn_gain[l], w_out[l]),
                           norm_mix_post[l])
        ks_l.append(sk_sb)
        vs_l.append(sv_sb)
        sts_l.append(st_s.astype(state_ret.dtype))

        xp = half_ffn(xp, norm_ffn2_pre[l], norm_ffn2_post[l], ffn2_gate[l], ffn2_up[l], ffn2_down[l])
        xs = half_ffn(xs, norm_ffn2_pre[l], norm_ffn2_post[l], ffn2_gate[l], ffn2_up[l], ffn2_down[l])

    k_prompt = jnp.stack(kp_l)
    v_prompt = jnp.stack(vp_l)
    ret_state_prompt = jnp.stack(stp_l)
    k_sample = jnp.stack(ks_l)
    v_sample = jnp.stack(vs_l)
    ret_state_sample = jnp.stack(sts_l)
    return (xp, xs, k_prompt, v_prompt, ret_state_prompt, k_sample, v_sample, ret_state_sample)
```

```python
import functools

import jax
import jax.numpy as jnp
from jax import lax
from jax.experimental import pallas as pl
from jax.experimental.pallas import tpu as pltpu

F32 = jnp.float32
BF16 = jnp.bfloat16

HEAD_DIM = 128
ROPE_BASE = 10000.0
NORM_EPS = 1e-6
GN_EPS = 1e-5
RET_CHUNK = 128
PAGE_SIZE = 128

V7X_VMEM_LIMIT_BYTES = 56 * 1024 * 1024
LANES = 128


def _params(n_axes):
    return pltpu.CompilerParams(
        dimension_semantics=("arbitrary",) * n_axes,
        vmem_limit_bytes=V7X_VMEM_LIMIT_BYTES,
    )


def _dot(a, b):
    return jnp.dot(a, b, preferred_element_type=F32)


def _dot_nt(a, b):
    return lax.dot_general(a, b, (((1,), (1,)), ((), ())), preferred_element_type=F32)


def _cast_kernel(w_ref, o_ref):
    o_ref[...] = w_ref[...].astype(BF16)


def _cast_bf16(w, rows=256):
    depth, k, n = w.shape
    return pl.pallas_call(
        _cast_kernel,
        out_shape=jax.ShapeDtypeStruct(w.shape, BF16),
        grid=(depth, k // rows),
        in_specs=[pl.BlockSpec((None, rows, n), lambda d, i: (d, i, 0))],
        out_specs=pl.BlockSpec((None, rows, n), lambda d, i: (d, i, 0)),
        compiler_params=_params(2),
        name="cast_bf16",
    )(w)


def _rms(x, g):
    return x * lax.rsqrt(jnp.mean(x * x, axis=-1, keepdims=True) + NORM_EPS) * g


def _norm_cast_kernel(x_ref, g_ref, h_ref):
    h_ref[...] = _rms(x_ref[...], g_ref[...]).astype(BF16)


def _norm_cast(x, g, rows):
    m, d = x.shape
    return pl.pallas_call(
        _norm_cast_kernel,
        out_shape=jax.ShapeDtypeStruct((m, d), BF16),
        grid=(m // rows,),
        in_specs=[pl.BlockSpec((rows, d), lambda i: (i, 0)),
                  pl.BlockSpec((1, d), lambda i: (0, 0))],
        out_specs=pl.BlockSpec((rows, d), lambda i: (i, 0)),
        compiler_params=_params(1),
        name="norm_cast",
    )(x, g.reshape(1, d))


def _resid_norm_kernel(x_ref, y_ref, gp_ref, gn_ref, xo_ref, h_ref, *, scale):
    xn = x_ref[...] + scale * _rms(y_ref[...], gp_ref[...])
    xo_ref[...] = xn
    h_ref[...] = _rms(xn, gn_ref[...]).astype(BF16)


def _resid_kernel(x_ref, y_ref, gp_ref, xo_ref, *, scale):
    xo_ref[...] = x_ref[...] + scale * _rms(y_ref[...], gp_ref[...])


def _resid_norm(x, y, g_post, g_next, scale, rows):
    m, d = x.shape
    row_spec = pl.BlockSpec((rows, d), lambda i: (i, 0))
    g_spec = pl.BlockSpec((1, d), lambda i: (0, 0))
    if g_next is None:
        return pl.pallas_call(
            functools.partial(_resid_kernel, scale=scale),
            out_shape=jax.ShapeDtypeStruct((m, d), F32),
            grid=(m // rows,),
            in_specs=[row_spec, row_spec, g_spec],
            out_specs=row_spec,
            compiler_params=_params(1),
            name="resid",
        )(x, y, g_post.reshape(1, d)), None
    return pl.pallas_call(
        functools.partial(_resid_norm_kernel, scale=scale),
        out_shape=(jax.ShapeDtypeStruct((m, d), F32), jax.ShapeDtypeStruct((m, d), BF16)),
        grid=(m // rows,),
        in_specs=[row_spec, row_spec, g_spec, g_spec],
        out_specs=(row_spec, row_spec),
        compiler_params=_params(1),
        name="resid_norm",
    )(x, y, g_post.reshape(1, d), g_next.reshape(1, d))


def _mm_kernel(x_ref, w_ref, o_ref):
    o_ref[...] = _dot(x_ref[...], w_ref[...]).astype(o_ref.dtype)


def _mm2_kernel(x1_ref, x2_ref, w1_ref, w2_ref, o_ref):
    acc = _dot(x1_ref[...], w1_ref[...]) + _dot(x2_ref[...], w2_ref[...])
    o_ref[...] = acc.astype(o_ref.dtype)


def _ffn_up_kernel(x_ref, wg_ref, wu_ref, o_ref):
    x = x_ref[...]
    g = _dot(x, wg_ref[...])
    u = _dot(x, wu_ref[...])
    o_ref[...] = (g * jax.nn.sigmoid(g) * u).astype(o_ref.dtype)


def _mm(x, w, layer, tm, tn, out_dtype):
    m, k = x.shape
    n = w.shape[2]
    return pl.pallas_call(
        _mm_kernel,
        out_shape=jax.ShapeDtypeStruct((m, n), out_dtype),
        grid=(m // tm, pl.cdiv(n, tn)),
        in_specs=[pl.BlockSpec((tm, k), lambda i, j: (i, 0)),
                  pl.BlockSpec((None, k, tn), lambda i, j: (layer, 0, j))],
        out_specs=pl.BlockSpec((tm, tn), lambda i, j: (i, j)),
        compiler_params=_params(2),
        name="mm",
    )(x, w)


def _mm_out(a_sb, a_r, w, layer, tm, tn):
    m, k = a_sb.shape
    n = w.shape[2]
    return pl.pallas_call(
        _mm2_kernel,
        out_shape=jax.ShapeDtypeStruct((m, n), F32),
        grid=(m // tm, n // tn),
        in_specs=[pl.BlockSpec((tm, k), lambda i, j: (i, 0)),
                  pl.BlockSpec((tm, k), lambda i, j: (i, 0)),
                  pl.BlockSpec((None, k, tn), lambda i, j: (layer, 0, j)),
                  pl.BlockSpec((None, k, tn), lambda i, j: (layer, 1, j))],
        out_specs=pl.BlockSpec((tm, tn), lambda i, j: (i, j)),
        compiler_params=_params(2),
        name="mm_out",
    )(a_sb, a_r, w, w)


def _ffn_up(x, wg, wu, layer, tm, tn):
    m, k = x.shape
    n = wg.shape[2]
    w_spec = pl.BlockSpec((None, k, tn), lambda i, j: (layer, 0, j))
    return pl.pallas_call(
        _ffn_up_kernel,
        out_shape=jax.ShapeDtypeStruct((m, n), BF16),
        grid=(m // tm, pl.cdiv(n, tn)),
        in_specs=[pl.BlockSpec((tm, k), lambda i, j: (i, 0)), w_spec, w_spec],
        out_specs=pl.BlockSpec((tm, tn), lambda i, j: (i, j)),
        compiler_params=_params(2),
        name="ffn_up",
    )(x, wg, wu)


def _rope_table_kernel(inv_ref, cos_ref, sin_ref, *, pos0, stride):
    shape = cos_ref.shape
    row = lax.broadcasted_iota(jnp.int32, shape, 0)
    lane = lax.broadcasted_iota(jnp.int32, shape, 1)
    pos = (pos0 + stride * row).astype(F32)
    ang = pos * inv_ref[...]
    cos_ref[...] = jnp.cos(ang)
    s = jnp.sin(ang)
    sin_ref[...] = jnp.where(lane < HEAD_DIM // 2, -s, s)


def _rope_tables(n_rows, pos0, stride):
    half = HEAD_DIM // 2
    inv = ROPE_BASE ** (-jnp.arange(half, dtype=F32) / half)
    inv = jnp.concatenate([inv, inv]).reshape(1, HEAD_DIM)
    out = jax.ShapeDtypeStruct((n_rows, HEAD_DIM), F32)
    return pl.pallas_call(
        functools.partial(_rope_table_kernel, pos0=pos0, stride=stride),
        out_shape=(out, out),
        name="rope_tables",
    )(inv)


def _rope(x, cos, sin_signed):
    return x * cos + pltpu.roll(x, HEAD_DIM // 2, 1) * sin_signed


def _log_sigmoids(z):
    t = jnp.log1p(jnp.exp(-jnp.abs(z)))
    return jnp.minimum(z, 0.0) - t, -jnp.maximum(z, 0.0) - t


def _later_sum(l, upper):
    hi = l.astype(BF16)
    lo = (l - hi.astype(F32)).astype(BF16)
    return _dot(hi, upper) + _dot(lo, upper)


def _strict_upper(n):
    r = lax.broadcasted_iota(jnp.int32, (n, n), 0)
    c = lax.broadcasted_iota(jnp.int32, (n, n), 1)
    return jnp.where(r > c, 1.0, 0.0).astype(BF16)


def _sb_prompt_kernel(bias_ref, q_ref, k_ref, v_ref, gain_ref, o_ref, kbf, vbf,
                      *, tq, tk, scale):
    h = pl.program_id(1)
    qi = pl.program_id(2)

    @pl.when(qi == 0)
    def _():
        kbf[...] = k_ref[...].astype(BF16)
        vbf[...] = v_ref[...].astype(BF16)

    q = q_ref[...].astype(BF16)
    bias = bias_ref[h]
    upper = _strict_upper(LANES)
    qpos = qi * tq + lax.broadcasted_iota(jnp.int32, (tq, LANES), 0)
    col = lax.broadcasted_iota(jnp.int32, (tq, LANES), 1)
    n_sub = tk // LANES
    n_blk = ((qi + 1) * tq + tk - 1) // tk

    def body(t, carry):
        run, acc = carry
        k0 = pl.multiple_of((n_blk - 1 - t) * tk, tk)
        kb = kbf[pl.ds(k0, tk), :]
        vb = vbf[pl.ds(k0, tk), :]
        z = _dot_nt(q, kb) * scale + bias
        ws = [None] * n_sub
        for s in reversed(range(n_sub)):
            zs = z[:, s * LANES:(s + 1) * LANES]
            valid = (k0 + s * LANES + col) < qpos
            ls, l1m = _log_sigmoids(zs)
            l1m = jnp.where(valid, l1m, 0.0)
            later = _later_sum(l1m, upper) + run
            ws[s] = jnp.where(valid, jnp.exp(ls + later), 0.0).astype(BF16)
            run = run + jnp.sum(l1m, axis=-1, keepdims=True)
        w = jnp.concatenate(ws, axis=1) if n_sub > 1 else ws[0]
        return run, acc + _dot(w, vb)

    run0 = jnp.zeros((tq, 1), F32)
    acc0 = jnp.zeros((tq, HEAD_DIM), F32)
    _, acc = lax.fori_loop(0, n_blk, body, (run0, acc0))
    o_ref[...] = _rms(acc, gain_ref[...]).astype(o_ref.dtype)


def _sb_prompt(proj, bias, gain, batch, seq, n_heads, tq=256, tk=512):
    nq = seq // tq
    kernel = functools.partial(_sb_prompt_kernel, tq=tq, tk=tk, scale=HEAD_DIM ** -0.5)
    return pl.pallas_call(
        kernel,
        out_shape=jax.ShapeDtypeStruct((batch * seq, n_heads * HEAD_DIM), BF16),
        grid=(batch, n_heads, nq),
        in_specs=[
            pl.BlockSpec(memory_space=pltpu.SMEM),
            pl.BlockSpec((tq, HEAD_DIM), lambda b, h, i: (b * nq + i, h)),
            pl.BlockSpec((seq, HEAD_DIM), lambda b, h, i: (b, n_heads + h)),
            pl.BlockSpec((seq, HEAD_DIM), lambda b, h, i: (b, 2 * n_heads + h)),
            pl.BlockSpec((None, 1, HEAD_DIM), lambda b, h, i: (h, 0, 0)),
        ],
        out_specs=pl.BlockSpec((tq, HEAD_DIM), lambda b, h, i: (b * nq + i, h)),
        scratch_shapes=[pltpu.VMEM((seq, HEAD_DIM), BF16), pltpu.VMEM((seq, HEAD_DIM), BF16)],
        compiler_params=_params(3),
        name="sb_prompt",
    )(bias, proj, proj, proj, gain.reshape(n_heads, 1, HEAD_DIM))


def _group_norm(o, g):
    mu = jnp.mean(o, axis=-1, keepdims=True)
    d = o - mu
    var = jnp.mean(d * d, axis=-1, keepdims=True)
    return d * lax.rsqrt(var + GN_EPS) * g


def _ret_prompt_kernel(logg_ref, q_ref, k_ref, v_ref, g_ref, cos_ref, sin_ref, gain_ref,
                       o_ref, st_ref, *, chunk, n_chunks, scale):
    h = pl.program_id(1)
    lg = logg_ref[h]
    ri = lax.broadcasted_iota(jnp.int32, (chunk, chunk), 0)
    ci = lax.broadcasted_iota(jnp.int32, (chunk, chunk), 1)
    diff = (ri - ci).astype(F32)
    decay = jnp.where(diff >= 0, jnp.exp(lg * jnp.maximum(diff, 0.0)), 0.0)
    idx = lax.broadcasted_iota(jnp.int32, (chunk, 1), 0).astype(F32)
    q_dec = jnp.exp(lg * (idx + 1.0))
    k_dec = jnp.exp(lg * (chunk - 1.0 - idx))
    s_dec = jnp.exp(jnp.full((1, 1), lg * chunk, F32))
    gain = gain_ref[...]

    state = jnp.zeros((HEAD_DIM, HEAD_DIM), F32)
    for n in range(n_chunks):
        rows = slice(n * chunk, (n + 1) * chunk)
        c = cos_ref[rows, :]
        s = sin_ref[rows, :]
        qf = _rope(q_ref[rows, :], c, s)
        kf = _rope(k_ref[rows, :], c, s) * scale
        qb = qf.astype(BF16)
        vb = v_ref[rows, :].astype(BF16)
        scores = _dot_nt(qb, kf.astype(BF16)) * decay
        o = _dot(scores.astype(BF16), vb) + _dot(qb, state.astype(BF16)) * q_dec
        kd_t = (kf * k_dec).T.astype(BF16)
        state = state * s_dec + _dot(kd_t, vb)
        gate = g_ref[rows, :]
        o_ref[rows, :] = (gate * jax.nn.sigmoid(gate) * _group_norm(o, gain)).astype(o_ref.dtype)
    st_ref[...] = state


def _ret_prompt(proj, log_g, cos, sin, gain, batch, seq, n_heads, col0):
    n_chunks = seq // RET_CHUNK
    kernel = functools.partial(_ret_prompt_kernel, chunk=RET_CHUNK, n_chunks=n_chunks,
                               scale=HEAD_DIM ** -0.5)

    def head_spec(group):
        return pl.BlockSpec((seq, HEAD_DIM), lambda b, h: (b, col0 + group * n_heads + h))

    table_spec = pl.BlockSpec((seq, HEAD_DIM), lambda b, h: (0, 0))
    return pl.pallas_call(
        kernel,
        out_shape=(jax.ShapeDtypeStruct((batch * seq, n_heads * HEAD_DIM), BF16),
                   jax.ShapeDtypeStruct((batch, n_heads, HEAD_DIM, HEAD_DIM), F32)),
        grid=(batch, n_heads),
        in_specs=[pl.BlockSpec(memory_space=pltpu.SMEM),
                  head_spec(0), head_spec(1), head_spec(2), head_spec(3),
                  table_spec, table_spec,
                  pl.BlockSpec((None, 1, HEAD_DIM), lambda b, h: (h, 0, 0))],
        out_specs=(pl.BlockSpec((seq, HEAD_DIM), lambda b, h: (b, h)),
                   pl.BlockSpec((None, None, HEAD_DIM, HEAD_DIM), lambda b, h: (b, h, 0, 0))),
        compiler_params=_params(2),
        name="ret_prompt",
    )(log_g, proj, proj, proj, proj, cos, sin, gain.reshape(n_heads, 1, HEAD_DIM))


def _sb_decode_tokens(k3, v3, qt, eye, bias, run, k_pos0, q_pos, scale):
    n_tok, n_heads, _ = k3.shape
    k2 = k3.reshape(n_tok * n_heads, HEAD_DIM).astype(BF16)
    v2 = v3.reshape(n_tok * n_heads, HEAD_DIM).astype(BF16)
    zt = _dot(k2, qt).reshape(n_tok, n_heads, n_heads)
    z = jnp.sum(jnp.where(eye[None], zt, 0.0), axis=1) * scale + bias
    k_pos = k_pos0 + lax.broadcasted_iota(jnp.int32, (n_tok, n_heads), 0)
    valid = k_pos < q_pos
    ls, l1m = _log_sigmoids(z)
    l1m = jnp.where(valid, l1m, 0.0)
    if n_tok > 1:
        r = lax.broadcasted_iota(jnp.int32, (n_tok, n_tok), 0)
        c = lax.broadcasted_iota(jnp.int32, (n_tok, n_tok), 1)
        later_tok = jnp.where(c > r, 1.0, 0.0).astype(BF16)
        hi = l1m.astype(BF16)
        lo = (l1m - hi.astype(F32)).astype(BF16)
        later = _dot(later_tok, hi) + _dot(later_tok, lo) + run
    else:
        later = run
    w = jnp.where(valid, jnp.exp(ls + later), 0.0)
    a = jnp.where(eye[None], w[:, None, :], 0.0).astype(BF16)
    a2 = a.reshape(n_tok * n_heads, n_heads)
    o = lax.dot_general(a2, v2, (((0,), (0,)), ((), ())), preferred_element_type=F32)
    return o, run + jnp.sum(l1m, axis=0, keepdims=True)


def _sb_decode_kernel(pt_ref, q_ref, kn_ref, vn_ref, bias_ref, gain_ref, kp_ref, vp_ref,
                      o_ref, qt_sc, run_sc, acc_sc, *, n_pages, n_heads, scale):
    del pt_ref
    p = pl.program_id(1)
    q_pos = n_pages * PAGE_SIZE
    bias = bias_ref[...]
    eye = (lax.broadcasted_iota(jnp.int32, (n_heads, n_heads), 0)
           == lax.broadcasted_iota(jnp.int32, (n_heads, n_heads), 1))

    @pl.when(p == 0)
    def _():
        qt = q_ref[...].T.astype(BF16)
        qt_sc[...] = qt
        o, run = _sb_decode_tokens(kn_ref[...][None], vn_ref[...][None], qt, eye, bias,
                                   jnp.zeros((1, n_heads), F32), q_pos, q_pos, scale)
        acc_sc[...] = o
        run_sc[...] = run

    page = n_pages - 1 - p
    o, run = _sb_decode_tokens(kp_ref[...], vp_ref[...], qt_sc[...], eye, bias, run_sc[...],
                               page * PAGE_SIZE, q_pos, scale)
    acc_sc[...] += o
    run_sc[...] = run

    @pl.when(p == n_pages - 1)
    def _():
        o_ref[...] = _rms(acc_sc[...], gain_ref[...])


def _sb_decode(page_table, q, k_new, v_new, bias, gain, cache_k, cache_v, layer):
    n_seq, n_pages = page_table.shape
    n_heads = q.shape[1]
    kernel = functools.partial(_sb_decode_kernel, n_pages=n_pages, n_heads=n_heads,
                               scale=HEAD_DIM ** -0.5)
    tok_spec = pl.BlockSpec((None, n_heads, HEAD_DIM), lambda b, p, pt: (b, 0, 0))
    page_spec = pl.BlockSpec((None, None, PAGE_SIZE, n_heads, HEAD_DIM),
                             lambda b, p, pt: (layer, pt[b, n_pages - 1 - p], 0, 0, 0))
    return pl.pallas_call(
        kernel,
        out_shape=jax.ShapeDtypeStruct((n_seq, n_heads, HEAD_DIM), F32),
        grid_spec=pltpu.PrefetchScalarGridSpec(
            num_scalar_prefetch=1,
            grid=(n_seq, n_pages),
            in_specs=[tok_spec, tok_spec, tok_spec,
                      pl.BlockSpec((1, n_heads), lambda b, p, pt: (0, 0)),
                      pl.BlockSpec((n_heads, HEAD_DIM), lambda b, p, pt: (0, 0)),
                      page_spec, page_spec],
            out_specs=tok_spec,
            scratch_shapes=[pltpu.VMEM((HEAD_DIM, n_heads), BF16),
                            pltpu.VMEM((1, n_heads), F32),
                            pltpu.VMEM((n_heads, HEAD_DIM), F32)],
        ),
        compiler_params=_params(2),
        name="sb_decode",
    )(page_table, q, k_new, v_new, bias, gain, cache_k, cache_v)


def _ret_decode_kernel(q_ref, k_ref, v_ref, g_ref, cos_ref, sin_ref, gain_ref, logg_ref, st_ref,
                       o_ref, sto_ref, *, n_heads, scale):
    c = cos_ref[...]
    s = sin_ref[...]
    qf = _rope(q_ref[...], c, s)
    kf = _rope(k_ref[...], c, s) * scale
    v = v_ref[...]
    gamma = jnp.exp(logg_ref[...])
    qt = qf.T
    kt = kf.T
    score = jnp.sum(qf * kf, axis=-1, keepdims=True)
    rows = []
    for h in range(n_heads):
        st = st_ref[h]
        g_h = gamma[h:h + 1, :]
        v_h = v[h:h + 1, :]
        read = jnp.sum(qt[:, h:h + 1] * st, axis=0, keepdims=True)
        rows.append(score[h:h + 1, :] * v_h + read * g_h)
        sto_ref[h] = st * g_h + kt[:, h:h + 1] * v_h
    o = jnp.concatenate(rows, axis=0)
    gate = g_ref[...]
    o_ref[...] = gate * jax.nn.sigmoid(gate) * _group_norm(o, gain_ref[...])


def _ret_decode(q, k, v, g, cos, sin, gain, log_g, state, layer):
    n_seq, n_heads, _ = q.shape
    kernel = functools.partial(_ret_decode_kernel, n_heads=n_heads, scale=HEAD_DIM ** -0.5)
    head_spec = pl.BlockSpec((None, n_heads, HEAD_DIM), lambda b: (b, 0, 0))
    return pl.pallas_call(
        kernel,
        out_shape=(jax.ShapeDtypeStruct((n_seq, n_heads, HEAD_DIM), F32),
                   jax.ShapeDtypeStruct((n_seq, n_heads, HEAD_DIM, HEAD_DIM), F32)),
        grid=(n_seq,),
        in_specs=[head_spec, head_spec, head_spec, head_spec,
                  pl.BlockSpec((1, HEAD_DIM), lambda b: (0, 0)),
                  pl.BlockSpec((1, HEAD_DIM), lambda b: (0, 0)),
                  pl.BlockSpec((n_heads, HEAD_DIM), lambda b: (0, 0)),
                  pl.BlockSpec((n_heads, 1), lambda b: (0, 0)),
                  pl.BlockSpec((None, None, n_heads, HEAD_DIM, HEAD_DIM),
                               lambda b: (layer, b, 0, 0, 0))],
        out_specs=(head_spec,
                   pl.BlockSpec((None, n_heads, HEAD_DIM, HEAD_DIM), lambda b: (b, 0, 0, 0))),
        compiler_params=_params(1),
        name="ret_decode",
    )(q, k, v, g, cos, sin, gain, log_g, state)


def kernel(x_prompt, x_sample, cache_k, cache_v, state_ret, page_table, w_in, w_out, sb_gain,
           gn_gain, sb_bias, ffn1_gate, ffn1_up, ffn1_down, ffn2_gate, ffn2_up, ffn2_down,
           norm_ffn1_pre, norm_ffn1_post, norm_mix_pre, norm_mix_post, norm_ffn2_pre,
           norm_ffn2_post):
    batch, seq, d_model = x_prompt.shape
    n_seq, dec_seq, _ = x_sample.shape
    depth = w_in.shape[0]
    n_sb = sb_gain.shape[1]
    n_ret = gn_gain.shape[1]
    sb_w = n_sb * HEAD_DIM
    ret_w = n_ret * HEAD_DIM
    m = batch * seq
    ms = n_seq * dec_seq
    past_len = page_table.shape[1] * PAGE_SIZE

    w_in_b = _cast_bf16(w_in)
    w_out_b = _cast_bf16(w_out)
    ffn_b = [tuple(_cast_bf16(w) for w in ws)
             for ws in ((ffn1_gate, ffn1_up, ffn1_down), (ffn2_gate, ffn2_up, ffn2_down))]
    ffn_post = (norm_ffn1_post, norm_ffn2_post)

    log_g = jnp.log1p(-(2.0 ** (-5.0 - jnp.arange(n_ret, dtype=F32))))
    cos_p, sin_p = _rope_tables(seq, 0, 1)
    cos_s, sin_s = _rope_tables(8, past_len, 0)

    xp = x_prompt.reshape(m, d_model)
    xs = x_sample.reshape(ms, d_model)
    rows_p, rows_s = 256, ms
    hp = _norm_cast(xp, norm_ffn1_pre[0], rows_p)
    hs = _norm_cast(xs, norm_ffn1_pre[0], rows_s)

    def half_ffn(x, h, which, layer, g_next, tm_up, tm_down, rows):
        wg, wu, wd = ffn_b[which]
        f = _ffn_up(h, wg, wu, layer, tm_up, 512)
        y = _mm(f, wd, layer, tm_down, 512, F32)
        return _resid_norm(x, y, ffn_post[which][layer], g_next, 0.5, rows)

    kp_l, vp_l, stp_l, ks_l, vs_l, sts_l = [], [], [], [], [], []
    for l in range(depth):
        xp, hp = half_ffn(xp, hp, 0, l, norm_mix_pre[l], 1024, 512, rows_p)
        xs, hs = half_ffn(xs, hs, 0, l, norm_mix_pre[l], ms, ms, rows_s)

        proj = _mm(hp, w_in_b, l, 1024, 512, F32)
        a_sb = _sb_prompt(proj, sb_bias[l], sb_gain[l], batch, seq, n_sb)
        a_r, st_p = _ret_prompt(proj, log_g, cos_p, sin_p, gn_gain[l], batch, seq, n_ret,
                                3 * n_sb)
        y = _mm_out(a_sb, a_r, w_out_b, l, 1024, 512)
        xp, hp = _resid_norm(xp, y, norm_mix_post[l], norm_ffn2_pre[l], 1.0, rows_p)
        kp_l.append(proj[:, sb_w:2 * sb_w].reshape(batch, seq, n_sb, HEAD_DIM))
        vp_l.append(proj[:, 2 * sb_w:3 * sb_w].reshape(batch, seq, n_sb, HEAD_DIM))
        stp_l.append(st_p)

        proj_s = _mm(hs, w_in_b, l, ms, 512, F32)
        q_s = proj_s[:, :sb_w].reshape(ms, n_sb, HEAD_DIM)
        k_s = proj_s[:, sb_w:2 * sb_w].reshape(ms, n_sb, HEAD_DIM)
        v_s = proj_s[:, 2 * sb_w:3 * sb_w].reshape(ms, n_sb, HEAD_DIM)
        a_sb_s = _sb_decode(page_table, q_s, k_s, v_s, sb_bias[l].reshape(1, n_sb),
                            sb_gain[l], cache_k, cache_v, l)
        r0 = 3 * sb_w
        parts = [proj_s[:, r0 + i * ret_w:r0 + (i + 1) * ret_w].reshape(ms, n_ret, HEAD_DIM)
                 for i in range(4)]
        a_r_s, st_s = _ret_decode(parts[0], parts[1], parts[2], parts[3], cos_s[:1], sin_s[:1],
                                  gn_gain[l], log_g.reshape(n_ret, 1), state_ret, l)
        y = _mm_out(a_sb_s.reshape(ms, sb_w).astype(BF16), a_r_s.reshape(ms, ret_w).astype(BF16),
                    w_out_b, l, ms, 512)
        xs, hs = _resid_norm(xs, y, norm_mix_post[l], norm_ffn2_pre[l], 1.0, rows_s)
        ks_l.append(k_s.reshape(n_seq, dec_seq, n_sb, HEAD_DIM))
        vs_l.append(v_s.reshape(n_seq, dec_seq, n_sb, HEAD_DIM))
        sts_l.append(st_s)

        g_next = norm_ffn1_pre[l + 1] if l + 1 < depth else None
        xp, hp = half_ffn(xp, hp, 1, l, g_next, 1024, 512, rows_p)
        xs, hs = half_ffn(xs, hs, 1, l, g_next, ms, ms, rows_s)

    return (xp.reshape(batch, seq, d_model), xs.reshape(n_seq, dec_seq, d_model),
            jnp.stack(kp_l), jnp.stack(vp_l), jnp.stack(stp_l),
            jnp.stack(ks_l), jnp.stack(vs_l), jnp.stack(sts_l))
```

```python
import functools

import jax
import jax.numpy as jnp
from jax import lax
from jax.experimental import pallas as pl
from jax.experimental.pallas import tpu as pltpu

F32 = jnp.float32
BF16 = jnp.bfloat16

HEAD_DIM = 128
ROPE_BASE = 10000.0
NORM_EPS = 1e-6
GN_EPS = 1e-5
RET_CHUNK = 128
PAGE_SIZE = 128

V7X_VMEM_LIMIT_BYTES = 56 * 1024 * 1024
LANES = 128


def _params(n_axes):
    return pltpu.CompilerParams(
        dimension_semantics=("arbitrary",) * n_axes,
        vmem_limit_bytes=V7X_VMEM_LIMIT_BYTES,
    )


def _dot(a, b):
    return jnp.dot(a, b, preferred_element_type=F32)


def _dot_nt(a, b):
    return lax.dot_general(a, b, (((1,), (1,)), ((), ())), preferred_element_type=F32)


def _cast_kernel(w_ref, o_ref):
    o_ref[...] = w_ref[...].astype(BF16)


def _cast_bf16(w, rows=256):
    depth, k, n = w.shape
    return pl.pallas_call(
        _cast_kernel,
        out_shape=jax.ShapeDtypeStruct(w.shape, BF16),
        grid=(depth, k // rows),
        in_specs=[pl.BlockSpec((None, rows, n), lambda d, i: (d, i, 0))],
        out_specs=pl.BlockSpec((None, rows, n), lambda d, i: (d, i, 0)),
        compiler_params=_params(2),
        name="cast_bf16",
    )(w)


def _rms(x, g):
    return x * lax.rsqrt(jnp.mean(x * x, axis=-1, keepdims=True) + NORM_EPS) * g


def _norm_cast_kernel(x_ref, g_ref, h_ref):
    h_ref[...] = _rms(x_ref[...], g_ref[...]).astype(BF16)


def _norm_cast(x, g, rows):
    m, d = x.shape
    return pl.pallas_call(
        _norm_cast_kernel,
        out_shape=jax.ShapeDtypeStruct((m, d), BF16),
        grid=(m // rows,),
        in_specs=[pl.BlockSpec((rows, d), lambda i: (i, 0)),
                  pl.BlockSpec((1, d), lambda i: (0, 0))],
        out_specs=pl.BlockSpec((rows, d), lambda i: (i, 0)),
        compiler_params=_params(1),
        name="norm_cast",
    )(x, g.reshape(1, d))


def _resid_norm_kernel(x_ref, y_ref, gp_ref, gn_ref, xo_ref, h_ref, *, scale):
    xn = x_ref[...] + scale * _rms(y_ref[...], gp_ref[...])
    xo_ref[...] = xn
    h_ref[...] = _rms(xn, gn_ref[...]).astype(BF16)


def _resid_kernel(x_ref, y_ref, gp_ref, xo_ref, *, scale):
    xo_ref[...] = x_ref[...] + scale * _rms(y_ref[...], gp_ref[...])


def _resid_norm(x, y, g_post, g_next, scale, rows):
    m, d = x.shape
    row_spec = pl.BlockSpec((rows, d), lambda i: (i, 0))
    g_spec = pl.BlockSpec((1, d), lambda i: (0, 0))
    if g_next is None:
        return pl.pallas_call(
            functools.partial(_resid_kernel, scale=scale),
            out_shape=jax.ShapeDtypeStruct((m, d), F32),
            grid=(m // rows,),
            in_specs=[row_spec, row_spec, g_spec],
            out_specs=row_spec,
            compiler_params=_params(1),
            name="resid",
        )(x, y, g_post.reshape(1, d)), None
    return pl.pallas_call(
        functools.partial(_resid_norm_kernel, scale=scale),
        out_shape=(jax.ShapeDtypeStruct((m, d), F32), jax.ShapeDtypeStruct((m, d), BF16)),
        grid=(m // rows,),
        in_specs=[row_spec, row_spec, g_spec, g_spec],
        out_specs=(row_spec, row_spec),
        compiler_params=_params(1),
        name="resid_norm",
    )(x, y, g_post.reshape(1, d), g_next.reshape(1, d))


def _mm_kernel(x_ref, w_ref, o_ref):
    o_ref[...] = _dot(x_ref[...], w_ref[...]).astype(o_ref.dtype)


def _mm2_kernel(x1_ref, x2_ref, w1_ref, w2_ref, o_ref):
    acc = _dot(x1_ref[...], w1_ref[...]) + _dot(x2_ref[...], w2_ref[...])
    o_ref[...] = acc.astype(o_ref.dtype)


def _ffn_up_kernel(x_ref, wg_ref, wu_ref, o_ref):
    x = x_ref[...]
    g = _dot(x, wg_ref[...])
    u = _dot(x, wu_ref[...])
    o_ref[...] = (g * jax.nn.sigmoid(g) * u).astype(o_ref.dtype)


def _mm(x, w, layer, tm, tn, out_dtype):
    m, k = x.shape
    n = w.shape[2]
    return pl.pallas_call(
        _mm_kernel,
        out_shape=jax.ShapeDtypeStruct((m, n), out_dtype),
        grid=(m // tm, pl.cdiv(n, tn)),
        in_specs=[pl.BlockSpec((tm, k), lambda i, j: (i, 0)),
                  pl.BlockSpec((None, k, tn), lambda i, j: (layer, 0, j))],
        out_specs=pl.BlockSpec((tm, tn), lambda i, j: (i, j)),
        compiler_params=_params(2),
        name="mm",
    )(x, w)


def _mm_out(a_sb, a_r, w, layer, tm, tn):
    m, k = a_sb.shape
    n = w.shape[2]
    return pl.pallas_call(
        _mm2_kernel,
        out_shape=jax.ShapeDtypeStruct((m, n), F32),
        grid=(m // tm, n // tn),
        in_specs=[pl.BlockSpec((tm, k), lambda i, j: (i, 0)),
                  pl.BlockSpec((tm, k), lambda i, j: (i, 0)),
                  pl.BlockSpec((None, k, tn), lambda i, j: (layer, 0, j)),
                  pl.BlockSpec((None, k, tn), lambda i, j: (layer, 1, j))],
        out_specs=pl.BlockSpec((tm, tn), lambda i, j: (i, j)),
        compiler_params=_params(2),
        name="mm_out",
    )(a_sb, a_r, w, w)


def _ffn_up(x, wg, wu, layer, tm, tn):
    m, k = x.shape
    n = wg.shape[2]
    w_spec = pl.BlockSpec((None, k, tn), lambda i, j: (layer, 0, j))
    return pl.pallas_call(
        _ffn_up_kernel,
        out_shape=jax.ShapeDtypeStruct((m, n), BF16),
        grid=(m // tm, pl.cdiv(n, tn)),
        in_specs=[pl.BlockSpec((tm, k), lambda i, j: (i, 0)), w_spec, w_spec],
        out_specs=pl.BlockSpec((tm, tn), lambda i, j: (i, j)),
        compiler_params=_params(2),
        name="ffn_up",
    )(x, wg, wu)


def _rope_table_kernel(inv_ref, cos_ref, sin_ref, *, pos0, stride):
    shape = cos_ref.shape
    row = lax.broadcasted_iota(jnp.int32, shape, 0)
    lane = lax.broadcasted_iota(jnp.int32, shape, 1)
    pos = (pos0 + stride * row).astype(F32)
    ang = pos * inv_ref[...]
    cos_ref[...] = jnp.cos(ang)
    s = jnp.sin(ang)
    sin_ref[...] = jnp.where(lane < HEAD_DIM // 2, -s, s)


def _rope_tables(n_rows, pos0, stride):
    half = HEAD_DIM // 2
    inv = ROPE_BASE ** (-jnp.arange(half, dtype=F32) / half)
    inv = jnp.concatenate([inv, inv]).reshape(1, HEAD_DIM)
    out = jax.ShapeDtypeStruct((n_rows, HEAD_DIM), F32)
    return pl.pallas_call(
        functools.partial(_rope_table_kernel, pos0=pos0, stride=stride),
        out_shape=(out, out),
        name="rope_tables",
    )(inv)


def _rope(x, cos, sin_signed):
    return x * cos + pltpu.roll(x, HEAD_DIM // 2, 1) * sin_signed


LOG2E = 1.4426950408889634
SUBLANES = 8


def _log_sigmoids(z):
    t = jnp.log(1.0 + jnp.exp(-jnp.abs(z)))
    return jnp.minimum(z, 0.0) - t, -jnp.maximum(z, 0.0) - t


def _split_dot(a, ones_bf16):
    hi = a.astype(BF16)
    lo = (a - hi.astype(F32)).astype(BF16)
    return _dot(hi, ones_bf16) + _dot(lo, ones_bf16)


def _split_dot_left(ones_bf16, a):
    hi = a.astype(BF16)
    lo = (a - hi.astype(F32)).astype(BF16)
    return _dot(ones_bf16, hi) + _dot(ones_bf16, lo)


def _sb_prompt_kernel(bias_ref, q_ref, k_ref, v_ref, gain_ref, o_ref, kbf, vbf, acc_sc, zn_sc,
                      w_sc, *, tq, nh, scale):
    hg = pl.program_id(1)
    qi = pl.program_id(2)

    @pl.when(qi == 0)
    def _():
        kbf[...] = k_ref[...].astype(BF16)
        vbf[...] = v_ref[...].astype(BF16)

    r = lax.broadcasted_iota(jnp.int32, (LANES, LANES), 0)
    c = lax.broadcasted_iota(jnp.int32, (LANES, LANES), 1)
    this_and_later = jnp.where(r >= c, 1.0, 0.0).astype(BF16)
    n_sub = tq // LANES
    row = lax.broadcasted_iota(jnp.int32, (tq, LANES), 0)
    col = lax.broadcasted_iota(jnp.int32, (tq, LANES), 1)
    qs = [q_ref[:, h * HEAD_DIM:(h + 1) * HEAD_DIM].astype(BF16) for h in range(nh)]
    nbias = [-bias_ref[hg * nh + h] for h in range(nh)]

    def neg_scores(k0):
        return [_dot_nt(qs[h], kbf[pl.ds(k0, tq), h * HEAD_DIM:(h + 1) * HEAD_DIM]) * (-scale)
                + nbias[h] for h in range(nh)]

    def weights_times_values(k0):
        for h in range(nh):
            hs = slice(h * HEAD_DIM, (h + 1) * HEAD_DIM)
            acc_sc[:, hs] += _dot(w_sc[:, h * tq:(h + 1) * tq], vbf[pl.ds(k0, tq), hs])

    def block(zns, runs, masked, k_next):
        tiles = [(h, s) for h in range(nh) for s in reversed(range(n_sub))]
        valid = {s: (s * LANES + col) < row for s in range(n_sub)} if masked else None
        expo = {}
        runs = list(runs)
        for (h, s) in tiles:
            zn = zns[h][:, s * LANES:(s + 1) * LANES]
            l = jnp.minimum(zn, 0.0) - jnp.log(1.0 + jnp.exp2(jnp.abs(zn) * (-LOG2E)))
            if masked:
                l = jnp.where(valid[s], l, 0.0)
            expo[h, s] = _split_dot(l, this_and_later) + (runs[h] - zn)
            runs[h] = runs[h] + jnp.sum(l, axis=-1, keepdims=True)
        nxt = neg_scores(k_next)
        for h in range(nh):
            zn_sc[:, h * tq:(h + 1) * tq] = nxt[h]
        for (h, s) in tiles:
            w = jnp.exp2(expo[h, s] * LOG2E)
            if masked:
                w = jnp.where(valid[s], w, 0.0)
            w_sc[:, h * tq + s * LANES:h * tq + (s + 1) * LANES] = w.astype(BF16)
        return tuple(runs)

    def start(j):
        return pl.multiple_of(jnp.maximum(j, 0) * tq, tq)

    acc_sc[...] = jnp.zeros_like(acc_sc)
    runs = block(neg_scores(start(qi)), tuple(jnp.zeros((tq, 1), F32) for _ in range(nh)), True,
                 start(qi - 1))

    def body(t, runs):
        zns = [zn_sc[:, h * tq:(h + 1) * tq] for h in range(nh)]
        weights_times_values(start(qi - t))
        return block(zns, runs, False, start(qi - 2 - t))

    lax.fori_loop(0, qi, body, runs)
    weights_times_values(0)
    for h in range(nh):
        hs = slice(h * HEAD_DIM, (h + 1) * HEAD_DIM)
        o_ref[:, hs] = _rms(acc_sc[:, hs], gain_ref[:, hs]).astype(o_ref.dtype)


def _sb_prompt(proj, bias, gain, batch, seq, n_heads, tq=256, nh=4):
    nq = seq // tq
    ng = n_heads // nh
    w = nh * HEAD_DIM
    kernel = functools.partial(_sb_prompt_kernel, tq=tq, nh=nh, scale=HEAD_DIM ** -0.5)
    return pl.pallas_call(
        kernel,
        out_shape=jax.ShapeDtypeStruct((batch * seq, n_heads * HEAD_DIM), BF16),
        grid=(batch, ng, nq),
        in_specs=[
            pl.BlockSpec(memory_space=pltpu.SMEM),
            pl.BlockSpec((tq, w), lambda b, g, i: (b * nq + i, g)),
            pl.BlockSpec((seq, w), lambda b, g, i: (b, ng + g)),
            pl.BlockSpec((seq, w), lambda b, g, i: (b, 2 * ng + g)),
            pl.BlockSpec((1, w), lambda b, g, i: (0, g)),
        ],
        out_specs=pl.BlockSpec((tq, w), lambda b, g, i: (b * nq + i, g)),
        scratch_shapes=[pltpu.VMEM((seq, w), BF16), pltpu.VMEM((seq, w), BF16),
                        pltpu.VMEM((tq, w), F32), pltpu.VMEM((tq, nh * tq), F32),
                        pltpu.VMEM((tq, nh * tq), BF16)],
        compiler_params=_params(3),
        name="sb_prompt",
    )(bias, proj, proj, proj, gain.reshape(1, n_heads * HEAD_DIM))


def _group_norm(o, g):
    mu = jnp.mean(o, axis=-1, keepdims=True)
    d = o - mu
    var = jnp.mean(d * d, axis=-1, keepdims=True)
    return d * lax.rsqrt(var + GN_EPS) * g


def _ret_prompt_kernel(logg_ref, q_ref, k_ref, v_ref, g_ref, cos_ref, sin_ref, gain_ref,
                       o_ref, st_ref, *, chunk, n_chunks, scale):
    h = pl.program_id(1)
    lg = logg_ref[h]
    ri = lax.broadcasted_iota(jnp.int32, (chunk, chunk), 0)
    ci = lax.broadcasted_iota(jnp.int32, (chunk, chunk), 1)
    diff = (ri - ci).astype(F32)
    decay = jnp.where(diff >= 0, jnp.exp(lg * jnp.maximum(diff, 0.0)), 0.0)
    idx = lax.broadcasted_iota(jnp.int32, (chunk, 1), 0).astype(F32)
    q_dec = jnp.exp(lg * (idx + 1.0))
    k_dec = jnp.exp(lg * (chunk - 1.0 - idx))
    s_dec = jnp.exp(jnp.full((1, 1), lg * chunk, F32))
    gain = gain_ref[...]

    state = jnp.zeros((HEAD_DIM, HEAD_DIM), F32)
    for n in range(n_chunks):
        rows = slice(n * chunk, (n + 1) * chunk)
        c = cos_ref[rows, :]
        s = sin_ref[rows, :]
        qf = _rope(q_ref[rows, :], c, s)
        kf = _rope(k_ref[rows, :], c, s) * scale
        qb = qf.astype(BF16)
        vb = v_ref[rows, :].astype(BF16)
        scores = _dot_nt(qb, kf.astype(BF16)) * decay
        o = _dot(scores.astype(BF16), vb) + _dot(qb, state.astype(BF16)) * q_dec
        kd_t = (kf * k_dec).T.astype(BF16)
        state = state * s_dec + _dot(kd_t, vb)
        gate = g_ref[rows, :]
        o_ref[rows, :] = (gate * jax.nn.sigmoid(gate) * _group_norm(o, gain)).astype(o_ref.dtype)
    st_ref[...] = state


def _ret_prompt(proj, log_g, cos, sin, gain, batch, seq, n_heads, col0):
    n_chunks = seq // RET_CHUNK
    kernel = functools.partial(_ret_prompt_kernel, chunk=RET_CHUNK, n_chunks=n_chunks,
                               scale=HEAD_DIM ** -0.5)

    def head_spec(group):
        return pl.BlockSpec((seq, HEAD_DIM), lambda b, h: (b, col0 + group * n_heads + h))

    table_spec = pl.BlockSpec((seq, HEAD_DIM), lambda b, h: (0, 0))
    return pl.pallas_call(
        kernel,
        out_shape=(jax.ShapeDtypeStruct((batch * seq, n_heads * HEAD_DIM), BF16),
                   jax.ShapeDtypeStruct((batch, n_heads, HEAD_DIM, HEAD_DIM), F32)),
        grid=(batch, n_heads),
        in_specs=[pl.BlockSpec(memory_space=pltpu.SMEM),
                  head_spec(0), head_spec(1), head_spec(2), head_spec(3),
                  table_spec, table_spec,
                  pl.BlockSpec((None, 1, HEAD_DIM), lambda b, h: (h, 0, 0))],
        out_specs=(pl.BlockSpec((seq, HEAD_DIM), lambda b, h: (b, h)),
                   pl.BlockSpec((None, None, HEAD_DIM, HEAD_DIM), lambda b, h: (b, h, 0, 0))),
        compiler_params=_params(2),
        name="ret_prompt",
    )(log_g, proj, proj, proj, proj, cos, sin, gain.reshape(n_heads, 1, HEAD_DIM))


def _sb_decode_tokens(k3, v3, qt, eye, bias, run, k_pos0, q_pos, scale):
    n_tok, n_heads, _ = k3.shape
    k2 = k3.reshape(n_tok * n_heads, HEAD_DIM).astype(BF16)
    v2 = v3.reshape(n_tok * n_heads, HEAD_DIM).astype(BF16)
    zt = _dot(k2, qt).reshape(n_tok, n_heads, n_heads)
    z = jnp.sum(jnp.where(eye[None], zt, 0.0), axis=1) * scale + bias
    k_pos = k_pos0 + lax.broadcasted_iota(jnp.int32, (n_tok, n_heads), 0)
    valid = k_pos < q_pos
    ls, l1m = _log_sigmoids(z)
    l1m = jnp.where(valid, l1m, 0.0)
    if n_tok > 1:
        r = lax.broadcasted_iota(jnp.int32, (n_tok, n_tok), 0)
        c = lax.broadcasted_iota(jnp.int32, (n_tok, n_tok), 1)
        later_tok = jnp.where(c > r, 1.0, 0.0).astype(BF16)
        hi = l1m.astype(BF16)
        lo = (l1m - hi.astype(F32)).astype(BF16)
        later = _dot(later_tok, hi) + _dot(later_tok, lo) + run
    else:
        later = run
    w = jnp.where(valid, jnp.exp(ls + later), 0.0)
    a = jnp.where(eye[None], w[:, None, :], 0.0).astype(BF16)
    a2 = a.reshape(n_tok * n_heads, n_heads)
    o = lax.dot_general(a2, v2, (((0,), (0,)), ((), ())), preferred_element_type=F32)
    return o, run + jnp.sum(l1m, axis=0, keepdims=True)


def _sb_decode_pages(k3s, v3s, q_rep, bias_f, run_f, k_pos0s, q_pos, scale, n_heads):
    n_tok = k3s[0].shape[0]
    per_row = LANES // n_heads
    n_rows = n_tok // per_row
    hb_n = n_heads // SUBLANES
    shape3 = (per_row * hb_n, SUBLANES, LANES)
    g = lax.broadcasted_iota(jnp.int32, shape3, 0)
    i = lax.broadcasted_iota(jnp.int32, shape3, 1)
    c = lax.broadcasted_iota(jnp.int32, shape3, 2)
    keep = ((c // n_heads) == (g // hb_n)) & ((c % n_heads) == (g % hb_n) * SUBLANES + i)
    r = lax.broadcasted_iota(jnp.int32, (n_rows, LANES), 0)
    cl = lax.broadcasted_iota(jnp.int32, (n_rows, LANES), 1)
    a = lax.broadcasted_iota(jnp.int32, (LANES, LANES), 0)
    b = lax.broadcasted_iota(jnp.int32, (LANES, LANES), 1)
    same_head = (a % n_heads) == (b % n_heads)
    later_in_row = jnp.where(same_head & (a > b), 1.0, 0.0).astype(BF16)
    whole_row = jnp.where(same_head, 1.0, 0.0).astype(BF16)
    ra = lax.broadcasted_iota(jnp.int32, (n_rows, n_rows), 0)
    rb = lax.broadcasted_iota(jnp.int32, (n_rows, n_rows), 1)
    later_rows = jnp.where(rb > ra, 1.0, 0.0).astype(BF16)
    own = (lax.broadcasted_iota(jnp.int32, (n_heads, LANES), 0)
           == lax.broadcasted_iota(jnp.int32, (n_heads, LANES), 1) % n_heads)

    zs = []
    for k3 in k3s:
        zt = _dot(k3.reshape(n_tok * n_heads, HEAD_DIM).astype(BF16), q_rep)
        z4 = zt.reshape(n_rows, per_row * hb_n, SUBLANES, LANES)
        zs.append(jnp.sum(jnp.sum(jnp.where(keep[None], z4, 0.0), axis=1), axis=1) * scale + bias_f)
    parts = []
    for z, k_pos0 in zip(zs, k_pos0s):
        valid = (k_pos0 + r * per_row + cl // n_heads) < q_pos
        ls, l1m = _log_sigmoids(z)
        l1m = jnp.where(valid, l1m, 0.0)
        row_tot = _split_dot(l1m, whole_row)
        within = _split_dot(l1m, later_in_row) + _split_dot_left(later_rows, row_tot)
        parts.append((valid, ls + within, jnp.sum(row_tot, axis=0, keepdims=True)))
    o = None
    for (valid, expo, tot), v3 in zip(parts, v3s):
        w = jnp.where(valid, jnp.exp(expo + run_f), 0.0)
        run_f = run_f + tot
        pieces = [jnp.where(own, w[u:u + 1, :], 0.0).astype(BF16) for u in range(n_rows)]
        at = jnp.concatenate(pieces, axis=1)
        ov = _dot(at, v3.reshape(n_tok * n_heads, HEAD_DIM).astype(BF16))
        o = ov if o is None else o + ov
    return o, run_f


def _sb_decode_kernel(pt_ref, q_ref, kn_ref, vn_ref, bias_ref, biasf_ref, gain_ref, *rest,
                      n_pages, n_heads, scale, group):
    del pt_ref
    kp_refs, vp_refs = rest[:group], rest[group:2 * group]
    o_ref, qrep_sc, run_sc, acc_sc = rest[2 * group:]
    p = pl.program_id(1)
    q_pos = n_pages * PAGE_SIZE
    reps = LANES // n_heads

    @pl.when(p == 0)
    def _():
        qt = q_ref[...].T.astype(BF16)
        qrep_sc[...] = jnp.concatenate([qt] * reps, axis=1)
        eye = (lax.broadcasted_iota(jnp.int32, (n_heads, n_heads), 0)
               == lax.broadcasted_iota(jnp.int32, (n_heads, n_heads), 1))
        o, run = _sb_decode_tokens(kn_ref[...][None], vn_ref[...][None], qt, eye, bias_ref[...],
                                   jnp.zeros((1, n_heads), F32), q_pos, q_pos, scale)
        acc_sc[...] = o
        run_sc[...] = jnp.concatenate([run] * reps, axis=1)

    first = n_pages - 1 - p * group
    o, run = _sb_decode_pages([ref[...] for ref in kp_refs], [ref[...] for ref in vp_refs],
                              qrep_sc[...], biasf_ref[...], run_sc[...],
                              [(first - j) * PAGE_SIZE for j in range(group)], q_pos, scale,
                              n_heads)
    acc_sc[...] += o
    run_sc[...] = run

    @pl.when(p == n_pages // group - 1)
    def _():
        o_ref[...] = _rms(acc_sc[...], gain_ref[...])


def _sb_decode(page_table, q, k_new, v_new, bias, gain, cache_k, cache_v, layer, group=4):
    n_seq, n_pages = page_table.shape
    n_heads = q.shape[1]
    assert n_pages % group == 0 and LANES % n_heads == 0 and n_heads % SUBLANES == 0
    kernel = functools.partial(_sb_decode_kernel, n_pages=n_pages, n_heads=n_heads,
                               scale=HEAD_DIM ** -0.5, group=group)
    tok_spec = pl.BlockSpec((None, n_heads, HEAD_DIM), lambda b, p, pt: (b, 0, 0))

    def page_spec(j):
        return pl.BlockSpec((None, None, PAGE_SIZE, n_heads, HEAD_DIM),
                            lambda b, p, pt: (layer, pt[b, n_pages - 1 - (p * group + j)], 0, 0, 0))

    pages = [page_spec(j) for j in range(group)]
    bias_f = jnp.tile(bias, (1, LANES // n_heads))
    return pl.pallas_call(
        kernel,
        out_shape=jax.ShapeDtypeStruct((n_seq, n_heads, HEAD_DIM), F32),
        grid_spec=pltpu.PrefetchScalarGridSpec(
            num_scalar_prefetch=1,
            grid=(n_seq, n_pages // group),
            in_specs=[tok_spec, tok_spec, tok_spec,
                      pl.BlockSpec((1, n_heads), lambda b, p, pt: (0, 0)),
                      pl.BlockSpec((1, LANES), lambda b, p, pt: (0, 0)),
                      pl.BlockSpec((n_heads, HEAD_DIM), lambda b, p, pt: (0, 0))] + pages + pages,
            out_specs=tok_spec,
            scratch_shapes=[pltpu.VMEM((HEAD_DIM, LANES), BF16),
                            pltpu.VMEM((1, LANES), F32),
                            pltpu.VMEM((n_heads, HEAD_DIM), F32)],
        ),
        compiler_params=_params(2),
        name="sb_decode",
    )(page_table, q, k_new, v_new, bias, bias_f, gain, *([cache_k] * group), *([cache_v] * group))


def _ret_decode_kernel(q_ref, k_ref, v_ref, g_ref, cos_ref, sin_ref, gain_ref, logg_ref, st_ref,
                       o_ref, sto_ref, *, n_heads, scale):
    c = cos_ref[...]
    s = sin_ref[...]
    qf = _rope(q_ref[...], c, s)
    kf = _rope(k_ref[...], c, s) * scale
    v = v_ref[...]
    gamma = jnp.exp(logg_ref[...])
    qt = qf.T
    kt = kf.T
    score = jnp.sum(qf * kf, axis=-1, keepdims=True)
    rows = []
    for h in range(n_heads):
        st = st_ref[h]
        g_h = gamma[h:h + 1, :]
        v_h = v[h:h + 1, :]
        read = jnp.sum(qt[:, h:h + 1] * st, axis=0, keepdims=True)
        rows.append(score[h:h + 1, :] * v_h + read * g_h)
        sto_ref[h] = st * g_h + kt[:, h:h + 1] * v_h
    o = jnp.concatenate(rows, axis=0)
    gate = g_ref[...]
    o_ref[...] = gate * jax.nn.sigmoid(gate) * _group_norm(o, gain_ref[...])


def _ret_decode(q, k, v, g, cos, sin, gain, log_g, state, layer):
    n_seq, n_heads, _ = q.shape
    kernel = functools.partial(_ret_decode_kernel, n_heads=n_heads, scale=HEAD_DIM ** -0.5)
    head_spec = pl.BlockSpec((None, n_heads, HEAD_DIM), lambda b: (b, 0, 0))
    return pl.pallas_call(
        kernel,
        out_shape=(jax.ShapeDtypeStruct((n_seq, n_heads, HEAD_DIM), F32),
                   jax.ShapeDtypeStruct((n_seq, n_heads, HEAD_DIM, HEAD_DIM), F32)),
        grid=(n_seq,),
        in_specs=[head_spec, head_spec, head_spec, head_spec,
                  pl.BlockSpec((1, HEAD_DIM), lambda b: (0, 0)),
                  pl.BlockSpec((1, HEAD_DIM), lambda b: (0, 0)),
                  pl.BlockSpec((n_heads, HEAD_DIM), lambda b: (0, 0)),
                  pl.BlockSpec((n_heads, 1), lambda b: (0, 0)),
                  pl.BlockSpec((None, None, n_heads, HEAD_DIM, HEAD_DIM),
                               lambda b: (layer, b, 0, 0, 0))],
        out_specs=(head_spec,
                   pl.BlockSpec((None, n_heads, HEAD_DIM, HEAD_DIM), lambda b: (b, 0, 0, 0))),
        compiler_params=_params(1),
        name="ret_decode",
    )(q, k, v, g, cos, sin, gain, log_g, state)


def kernel(x_prompt, x_sample, cache_k, cache_v, state_ret, page_table, w_in, w_out, sb_gain,
           gn_gain, sb_bias, ffn1_gate, ffn1_up, ffn1_down, ffn2_gate, ffn2_up, ffn2_down,
           norm_ffn1_pre, norm_ffn1_post, norm_mix_pre, norm_mix_post, norm_ffn2_pre,
           norm_ffn2_post):
    batch, seq, d_model = x_prompt.shape
    n_seq, dec_seq, _ = x_sample.shape
    depth = w_in.shape[0]
    n_sb = sb_gain.shape[1]
    n_ret = gn_gain.shape[1]
    sb_w = n_sb * HEAD_DIM
    ret_w = n_ret * HEAD_DIM
    m = batch * seq
    ms = n_seq * dec_seq
    past_len = page_table.shape[1] * PAGE_SIZE

    w_in_b = _cast_bf16(w_in)
    w_out_b = _cast_bf16(w_out)
    ffn_b = [tuple(_cast_bf16(w) for w in ws)
             for ws in ((ffn1_gate, ffn1_up, ffn1_down), (ffn2_gate, ffn2_up, ffn2_down))]
    ffn_post = (norm_ffn1_post, norm_ffn2_post)

    log_g = jnp.log1p(-(2.0 ** (-5.0 - jnp.arange(n_ret, dtype=F32))))
    cos_p, sin_p = _rope_tables(seq, 0, 1)
    cos_s, sin_s = _rope_tables(8, past_len, 0)

    xp = x_prompt.reshape(m, d_model)
    xs = x_sample.reshape(ms, d_model)
    rows_p, rows_s = 256, ms
    hp = _norm_cast(xp, norm_ffn1_pre[0], rows_p)
    hs = _norm_cast(xs, norm_ffn1_pre[0], rows_s)

    def half_ffn(x, h, which, layer, g_next, tm_up, tm_down, rows):
        wg, wu, wd = ffn_b[which]
        f = _ffn_up(h, wg, wu, layer, tm_up, 512)
        y = _mm(f, wd, layer, tm_down, 512, F32)
        return _resid_norm(x, y, ffn_post[which][layer], g_next, 0.5, rows)

    kp_l, vp_l, stp_l, ks_l, vs_l, sts_l = [], [], [], [], [], []
    for l in range(depth):
        xp, hp = half_ffn(xp, hp, 0, l, norm_mix_pre[l], 1024, 512, rows_p)
        xs, hs = half_ffn(xs, hs, 0, l, norm_mix_pre[l], ms, ms, rows_s)

        proj = _mm(hp, w_in_b, l, 1024, 512, F32)
        a_sb = _sb_prompt(proj, sb_bias[l], sb_gain[l], batch, seq, n_sb)
        a_r, st_p = _ret_prompt(proj, log_g, cos_p, sin_p, gn_gain[l], batch, seq, n_ret,
                                3 * n_sb)
        y = _mm_out(a_sb, a_r, w_out_b, l, 1024, 512)
        xp, hp = _resid_norm(xp, y, norm_mix_post[l], norm_ffn2_pre[l], 1.0, rows_p)
        kp_l.append(proj[:, sb_w:2 * sb_w].reshape(batch, seq, n_sb, HEAD_DIM))
        vp_l.append(proj[:, 2 * sb_w:3 * sb_w].reshape(batch, seq, n_sb, HEAD_DIM))
        stp_l.append(st_p)

        proj_s = _mm(hs, w_in_b, l, ms, 512, F32)
        q_s = proj_s[:, :sb_w].reshape(ms, n_sb, HEAD_DIM)
        k_s = proj_s[:, sb_w:2 * sb_w].reshape(ms, n_sb, HEAD_DIM)
        v_s = proj_s[:, 2 * sb_w:3 * sb_w].reshape(ms, n_sb, HEAD_DIM)
        a_sb_s = _sb_decode(page_table, q_s, k_s, v_s, sb_bias[l].reshape(1, n_sb),
                            sb_gain[l], cache_k, cache_v, l)
        r0 = 3 * sb_w
        parts = [proj_s[:, r0 + i * ret_w:r0 + (i + 1) * ret_w].reshape(ms, n_ret, HEAD_DIM)
                 for i in range(4)]
        a_r_s, st_s = _ret_decode(parts[0], parts[1], parts[2], parts[3], cos_s[:1], sin_s[:1],
                                  gn_gain[l], log_g.reshape(n_ret, 1), state_ret, l)
        y = _mm_out(a_sb_s.reshape(ms, sb_w).astype(BF16), a_r_s.reshape(ms, ret_w).astype(BF16),
                    w_out_b, l, ms, 512)
        xs, hs = _resid_norm(xs, y, norm_mix_post[l], norm_ffn2_pre[l], 1.0, rows_s)
        ks_l.append(k_s.reshape(n_seq, dec_seq, n_sb, HEAD_DIM))
        vs_l.append(v_s.reshape(n_seq, dec_seq, n_sb, HEAD_DIM))
        sts_l.append(st_s)

        g_next = norm_ffn1_pre[l + 1] if l + 1 < depth else None
        xp, hp = half_ffn(xp, hp, 1, l, g_next, 1024, 512, rows_p)
        xs, hs = half_ffn(xs, hs, 1, l, g_next, ms, ms, rows_s)

    return (xp.reshape(batch, seq, d_model), xs.reshape(n_seq, dec_seq, d_model),
            jnp.stack(kp_l), jnp.stack(vp_l), jnp.stack(stp_l),
            jnp.stack(ks_l), jnp.stack(vs_l), jnp.stack(sts_l))
```

```python
import functools

import jax
import jax.numpy as jnp
from jax import lax
from jax.experimental import pallas as pl
from jax.experimental.pallas import tpu as pltpu

F32 = jnp.float32
BF16 = jnp.bfloat16

HEAD_DIM = 128
ROPE_BASE = 10000.0
NORM_EPS = 1e-6
GN_EPS = 1e-5
RET_CHUNK = 128
PAGE_SIZE = 128

V7X_VMEM_LIMIT_BYTES = 56 * 1024 * 1024
LANES = 128


def _params(n_axes):
    return pltpu.CompilerParams(
        dimension_semantics=("arbitrary",) * n_axes,
        vmem_limit_bytes=V7X_VMEM_LIMIT_BYTES,
    )


def _dot(a, b):
    return jnp.dot(a, b, preferred_element_type=F32)


def _dot_nt(a, b):
    return lax.dot_general(a, b, (((1,), (1,)), ((), ())), preferred_element_type=F32)


def _cast_kernel(w_ref, o_ref):
    o_ref[...] = w_ref[...].astype(BF16)


def _cast_bf16(w, rows=256):
    depth, k, n = w.shape
    return pl.pallas_call(
        _cast_kernel,
        out_shape=jax.ShapeDtypeStruct(w.shape, BF16),
        grid=(depth, k // rows),
        in_specs=[pl.BlockSpec((None, rows, n), lambda d, i: (d, i, 0))],
        out_specs=pl.BlockSpec((None, rows, n), lambda d, i: (d, i, 0)),
        compiler_params=_params(2),
        name="cast_bf16",
    )(w)


def _rms(x, g):
    return x * lax.rsqrt(jnp.mean(x * x, axis=-1, keepdims=True) + NORM_EPS) * g


def _norm_cast_kernel(x_ref, g_ref, h_ref):
    h_ref[...] = _rms(x_ref[...], g_ref[...]).astype(BF16)


def _norm_cast(x, g, rows):
    m, d = x.shape
    return pl.pallas_call(
        _norm_cast_kernel,
        out_shape=jax.ShapeDtypeStruct((m, d), BF16),
        grid=(m // rows,),
        in_specs=[pl.BlockSpec((rows, d), lambda i: (i, 0)),
                  pl.BlockSpec((1, d), lambda i: (0, 0))],
        out_specs=pl.BlockSpec((rows, d), lambda i: (i, 0)),
        compiler_params=_params(1),
        name="norm_cast",
    )(x, g.reshape(1, d))


def _resid_norm_kernel(x_ref, y_ref, gp_ref, gn_ref, xo_ref, h_ref, *, scale):
    xn = x_ref[...] + scale * _rms(y_ref[...], gp_ref[...])
    xo_ref[...] = xn
    h_ref[...] = _rms(xn, gn_ref[...]).astype(BF16)


def _resid_kernel(x_ref, y_ref, gp_ref, xo_ref, *, scale):
    xo_ref[...] = x_ref[...] + scale * _rms(y_ref[...], gp_ref[...])


def _resid_norm(x, y, g_post, g_next, scale, rows):
    m, d = x.shape
    row_spec = pl.BlockSpec((rows, d), lambda i: (i, 0))
    g_spec = pl.BlockSpec((1, d), lambda i: (0, 0))
    if g_next is None:
        return pl.pallas_call(
            functools.partial(_resid_kernel, scale=scale),
            out_shape=jax.ShapeDtypeStruct((m, d), F32),
            grid=(m // rows,),
            in_specs=[row_spec, row_spec, g_spec],
            out_specs=row_spec,
            compiler_params=_params(1),
            name="resid",
        )(x, y, g_post.reshape(1, d)), None
    return pl.pallas_call(
        functools.partial(_resid_norm_kernel, scale=scale),
        out_shape=(jax.ShapeDtypeStruct((m, d), F32), jax.ShapeDtypeStruct((m, d), BF16)),
        grid=(m // rows,),
        in_specs=[row_spec, row_spec, g_spec, g_spec],
        out_specs=(row_spec, row_spec),
        compiler_params=_params(1),
        name="resid_norm",
    )(x, y, g_post.reshape(1, d), g_next.reshape(1, d))


def _mm_kernel(x_ref, w_ref, o_ref):
    o_ref[...] = _dot(x_ref[...], w_ref[...]).astype(o_ref.dtype)


def _mm2_kernel(x1_ref, x2_ref, w1_ref, w2_ref, o_ref):
    acc = _dot(x1_ref[...], w1_ref[...]) + _dot(x2_ref[...], w2_ref[...])
    o_ref[...] = acc.astype(o_ref.dtype)


def _ffn_up_kernel(x_ref, wg_ref, wu_ref, o_ref):
    x = x_ref[...]
    g = _dot(x, wg_ref[...])
    u = _dot(x, wu_ref[...])
    o_ref[...] = (g * jax.nn.sigmoid(g) * u).astype(o_ref.dtype)


def _w_spec(w, layer, tn):
    if w.ndim == 2:
        return pl.BlockSpec((w.shape[0], tn), lambda i, j: (0, j))
    return pl.BlockSpec((None, w.shape[1], tn), lambda i, j: (layer, 0, j))


def _mm(x, w, layer, tm, tn, out_dtype):
    m, k = x.shape
    n = w.shape[-1]
    return pl.pallas_call(
        _mm_kernel,
        out_shape=jax.ShapeDtypeStruct((m, n), out_dtype),
        grid=(m // tm, pl.cdiv(n, tn)),
        in_specs=[pl.BlockSpec((tm, k), lambda i, j: (i, 0)), _w_spec(w, layer, tn)],
        out_specs=pl.BlockSpec((tm, tn), lambda i, j: (i, j)),
        compiler_params=_params(2),
        name="mm",
    )(x, w)


def _mm_ws_kernel(x_ref, w_ref, o_ref, wb_ref):
    @pl.when(pl.program_id(1) == 0)
    def _():
        wb_ref[...] = w_ref[...].astype(BF16)

    o_ref[...] = _dot(x_ref[...], wb_ref[...]).astype(o_ref.dtype)


def _ffn_up_ws_kernel(x_ref, wg_ref, wu_ref, o_ref, wgb_ref, wub_ref):
    @pl.when(pl.program_id(1) == 0)
    def _():
        wgb_ref[...] = wg_ref[...].astype(BF16)
        wub_ref[...] = wu_ref[...].astype(BF16)

    x = x_ref[...]
    g = _dot(x, wgb_ref[...])
    u = _dot(x, wub_ref[...])
    o_ref[...] = (g * jax.nn.sigmoid(g) * u).astype(o_ref.dtype)


def _mm_ws(x, w, layer, tm, tn, out_dtype):
    m, k = x.shape
    n = w.shape[2]
    assert n % tn == 0 and m % tm == 0
    return pl.pallas_call(
        _mm_ws_kernel,
        out_shape=(jax.ShapeDtypeStruct((m, n), out_dtype), jax.ShapeDtypeStruct((k, n), BF16)),
        grid=(n // tn, m // tm),
        in_specs=[pl.BlockSpec((tm, k), lambda j, i: (i, 0)),
                  pl.BlockSpec((None, k, tn), lambda j, i: (layer, 0, j))],
        out_specs=(pl.BlockSpec((tm, tn), lambda j, i: (i, j)),
                   pl.BlockSpec((k, tn), lambda j, i: (0, j))),
        compiler_params=_params(2),
        name="mm_ws",
    )(x, w)


def _ffn_up_ws(x, wg, wu, layer, tm, tn):
    m, k = x.shape
    n = wg.shape[2]
    assert n % tn == 0 and m % tm == 0
    w_spec = pl.BlockSpec((None, k, tn), lambda j, i: (layer, 0, j))
    wb_spec = pl.BlockSpec((k, tn), lambda j, i: (0, j))
    wb_shape = jax.ShapeDtypeStruct((k, n), BF16)
    return pl.pallas_call(
        _ffn_up_ws_kernel,
        out_shape=(jax.ShapeDtypeStruct((m, n), BF16), wb_shape, wb_shape),
        grid=(n // tn, m // tm),
        in_specs=[pl.BlockSpec((tm, k), lambda j, i: (i, 0)), w_spec, w_spec],
        out_specs=(pl.BlockSpec((tm, tn), lambda j, i: (i, j)), wb_spec, wb_spec),
        compiler_params=_params(2),
        name="ffn_up_ws",
    )(x, wg, wu)


def _mm_out(a_sb, a_r, w, layer, tm, tn):
    m, k = a_sb.shape
    n = w.shape[2]
    return pl.pallas_call(
        _mm2_kernel,
        out_shape=jax.ShapeDtypeStruct((m, n), F32),
        grid=(m // tm, n // tn),
        in_specs=[pl.BlockSpec((tm, k), lambda i, j: (i, 0)),
                  pl.BlockSpec((tm, k), lambda i, j: (i, 0)),
                  pl.BlockSpec((None, k, tn), lambda i, j: (layer, 0, j)),
                  pl.BlockSpec((None, k, tn), lambda i, j: (layer, 1, j))],
        out_specs=pl.BlockSpec((tm, tn), lambda i, j: (i, j)),
        compiler_params=_params(2),
        name="mm_out",
    )(a_sb, a_r, w, w)


def _ffn_up(x, wg, wu, layer, tm, tn):
    m, k = x.shape
    n = wg.shape[-1]
    w_spec = _w_spec(wg, layer, tn)
    return pl.pallas_call(
        _ffn_up_kernel,
        out_shape=jax.ShapeDtypeStruct((m, n), BF16),
        grid=(m // tm, pl.cdiv(n, tn)),
        in_specs=[pl.BlockSpec((tm, k), lambda i, j: (i, 0)), w_spec, w_spec],
        out_specs=pl.BlockSpec((tm, tn), lambda i, j: (i, j)),
        compiler_params=_params(2),
        name="ffn_up",
    )(x, wg, wu)


def _rope_table_kernel(inv_ref, cos_ref, sin_ref, *, pos0, stride):
    shape = cos_ref.shape
    row = lax.broadcasted_iota(jnp.int32, shape, 0)
    lane = lax.broadcasted_iota(jnp.int32, shape, 1)
    pos = (pos0 + stride * row).astype(F32)
    ang = pos * inv_ref[...]
    cos_ref[...] = jnp.cos(ang)
    s = jnp.sin(ang)
    sin_ref[...] = jnp.where(lane < HEAD_DIM // 2, -s, s)


def _rope_tables(n_rows, pos0, stride):
    half = HEAD_DIM // 2
    inv = ROPE_BASE ** (-jnp.arange(half, dtype=F32) / half)
    inv = jnp.concatenate([inv, inv]).reshape(1, HEAD_DIM)
    out = jax.ShapeDtypeStruct((n_rows, HEAD_DIM), F32)
    return pl.pallas_call(
        functools.partial(_rope_table_kernel, pos0=pos0, stride=stride),
        out_shape=(out, out),
        name="rope_tables",
    )(inv)


def _rope(x, cos, sin_signed):
    return x * cos + pltpu.roll(x, HEAD_DIM // 2, 1) * sin_signed


LOG2E = 1.4426950408889634
SUBLANES = 8


def _log_sigmoids(z):
    t = jnp.log(1.0 + jnp.exp(-jnp.abs(z)))
    return jnp.minimum(z, 0.0) - t, -jnp.maximum(z, 0.0) - t


def _split_dot(a, ones_bf16):
    hi = a.astype(BF16)
    lo = (a - hi.astype(F32)).astype(BF16)
    return _dot(hi, ones_bf16) + _dot(lo, ones_bf16)


def _split_dot_left(ones_bf16, a):
    hi = a.astype(BF16)
    lo = (a - hi.astype(F32)).astype(BF16)
    return _dot(ones_bf16, hi) + _dot(ones_bf16, lo)


def _sb_prompt_kernel(bias_ref, q_ref, k_ref, v_ref, gain_ref, o_ref, kbf, vbf, acc_sc, zn_sc,
                      w_sc, *, tq, nh, scale):
    hg = pl.program_id(1)
    qi = pl.program_id(2)

    @pl.when(qi == 0)
    def _():
        kbf[...] = k_ref[...].astype(BF16)
        vbf[...] = v_ref[...].astype(BF16)

    r = lax.broadcasted_iota(jnp.int32, (LANES, LANES), 0)
    c = lax.broadcasted_iota(jnp.int32, (LANES, LANES), 1)
    this_and_later = jnp.where(r >= c, 1.0, 0.0).astype(BF16)
    n_sub = tq // LANES
    row = lax.broadcasted_iota(jnp.int32, (tq, LANES), 0)
    col = lax.broadcasted_iota(jnp.int32, (tq, LANES), 1)
    qs = [q_ref[:, h * HEAD_DIM:(h + 1) * HEAD_DIM].astype(BF16) for h in range(nh)]
    nbias = [-bias_ref[hg * nh + h] for h in range(nh)]

    def neg_scores(k0):
        return [_dot_nt(qs[h], kbf[pl.ds(k0, tq), h * HEAD_DIM:(h + 1) * HEAD_DIM]) * (-scale)
                + nbias[h] for h in range(nh)]

    def weights_times_values(k0):
        for h in range(nh):
            hs = slice(h * HEAD_DIM, (h + 1) * HEAD_DIM)
            acc_sc[:, hs] += _dot(w_sc[:, h * tq:(h + 1) * tq], vbf[pl.ds(k0, tq), hs])

    def block(zns, runs, masked, k_next):
        tiles = [(h, s) for h in range(nh) for s in reversed(range(n_sub))]
        valid = {s: (s * LANES + col) < row for s in range(n_sub)} if masked else None
        expo = {}
        runs = list(runs)
        for (h, s) in tiles:
            zn = zns[h][:, s * LANES:(s + 1) * LANES]
            l = jnp.minimum(zn, 0.0) - jnp.log(1.0 + jnp.exp2(jnp.abs(zn) * (-LOG2E)))
            if masked:
                l = jnp.where(valid[s], l, 0.0)
            expo[h, s] = _split_dot(l, this_and_later) + (runs[h] - zn)
            runs[h] = runs[h] + jnp.sum(l, axis=-1, keepdims=True)
        nxt = neg_scores(k_next)
        for h in range(nh):
            zn_sc[:, h * tq:(h + 1) * tq] = nxt[h]
        for (h, s) in tiles:
            w = jnp.exp2(expo[h, s] * LOG2E)
            if masked:
                w = jnp.where(valid[s], w, 0.0)
            w_sc[:, h * tq + s * LANES:h * tq + (s + 1) * LANES] = w.astype(BF16)
        return tuple(runs)

    def start(j):
        return pl.multiple_of(jnp.maximum(j, 0) * tq, tq)

    acc_sc[...] = jnp.zeros_like(acc_sc)
    runs = block(neg_scores(start(qi)), tuple(jnp.zeros((tq, 1), F32) for _ in range(nh)), True,
                 start(qi - 1))

    def body(t, runs):
        zns = [zn_sc[:, h * tq:(h + 1) * tq] for h in range(nh)]
        weights_times_values(start(qi - t))
        return block(zns, runs, False, start(qi - 2 - t))

    lax.fori_loop(0, qi, body, runs)
    weights_times_values(0)
    for h in range(nh):
        hs = slice(h * HEAD_DIM, (h + 1) * HEAD_DIM)
        o_ref[:, hs] = _rms(acc_sc[:, hs], gain_ref[:, hs]).astype(o_ref.dtype)


def _sb_prompt(proj, bias, gain, batch, seq, n_heads, tq=256, nh=4):
    nq = seq // tq
    ng = n_heads // nh
    w = nh * HEAD_DIM
    kernel = functools.partial(_sb_prompt_kernel, tq=tq, nh=nh, scale=HEAD_DIM ** -0.5)
    return pl.pallas_call(
        kernel,
        out_shape=jax.ShapeDtypeStruct((batch * seq, n_heads * HEAD_DIM), BF16),
        grid=(batch, ng, nq),
        in_specs=[
            pl.BlockSpec(memory_space=pltpu.SMEM),
            pl.BlockSpec((tq, w), lambda b, g, i: (b * nq + i, g)),
            pl.BlockSpec((seq, w), lambda b, g, i: (b, ng + g)),
            pl.BlockSpec((seq, w), lambda b, g, i: (b, 2 * ng + g)),
            pl.BlockSpec((1, w), lambda b, g, i: (0, g)),
        ],
        out_specs=pl.BlockSpec((tq, w), lambda b, g, i: (b * nq + i, g)),
        scratch_shapes=[pltpu.VMEM((seq, w), BF16), pltpu.VMEM((seq, w), BF16),
                        pltpu.VMEM((tq, w), F32), pltpu.VMEM((tq, nh * tq), F32),
                        pltpu.VMEM((tq, nh * tq), BF16)],
        compiler_params=_params(3),
        name="sb_prompt",
    )(bias, proj, proj, proj, gain.reshape(1, n_heads * HEAD_DIM))


def _group_norm(o, g):
    mu = jnp.mean(o, axis=-1, keepdims=True)
    d = o - mu
    var = jnp.mean(d * d, axis=-1, keepdims=True)
    return d * lax.rsqrt(var + GN_EPS) * g


def _ret_prompt_kernel(logg_ref, q_ref, k_ref, v_ref, g_ref, cos_ref, sin_ref, gain_ref,
                       o_ref, st_ref, *, chunk, n_chunks, scale):
    h = pl.program_id(1)
    lg = logg_ref[h]
    ri = lax.broadcasted_iota(jnp.int32, (chunk, chunk), 0)
    ci = lax.broadcasted_iota(jnp.int32, (chunk, chunk), 1)
    diff = (ri - ci).astype(F32)
    decay = jnp.where(diff >= 0, jnp.exp(lg * jnp.maximum(diff, 0.0)), 0.0)
    idx = lax.broadcasted_iota(jnp.int32, (chunk, 1), 0).astype(F32)
    q_dec = jnp.exp(lg * (idx + 1.0))
    k_dec = jnp.exp(lg * (chunk - 1.0 - idx))
    s_dec = jnp.exp(jnp.full((1, 1), lg * chunk, F32))
    gain = gain_ref[...]

    state = jnp.zeros((HEAD_DIM, HEAD_DIM), F32)
    for n in range(n_chunks):
        rows = slice(n * chunk, (n + 1) * chunk)
        c = cos_ref[rows, :]
        s = sin_ref[rows, :]
        qf = _rope(q_ref[rows, :], c, s)
        kf = _rope(k_ref[rows, :], c, s) * scale
        qb = qf.astype(BF16)
        vb = v_ref[rows, :].astype(BF16)
        scores = _dot_nt(qb, kf.astype(BF16)) * decay
        o = _dot(scores.astype(BF16), vb) + _dot(qb, state.astype(BF16)) * q_dec
        kd_t = (kf * k_dec).T.astype(BF16)
        state = state * s_dec + _dot(kd_t, vb)
        gate = g_ref[rows, :]
        o_ref[rows, :] = (gate * jax.nn.sigmoid(gate) * _group_norm(o, gain)).astype(o_ref.dtype)
    st_ref[...] = state


def _ret_prompt(proj, log_g, cos, sin, gain, batch, seq, n_heads, col0):
    n_chunks = seq // RET_CHUNK
    kernel = functools.partial(_ret_prompt_kernel, chunk=RET_CHUNK, n_chunks=n_chunks,
                               scale=HEAD_DIM ** -0.5)

    def head_spec(group):
        return pl.BlockSpec((seq, HEAD_DIM), lambda b, h: (b, col0 + group * n_heads + h))

    table_spec = pl.BlockSpec((seq, HEAD_DIM), lambda b, h: (0, 0))
    return pl.pallas_call(
        kernel,
        out_shape=(jax.ShapeDtypeStruct((batch * seq, n_heads * HEAD_DIM), BF16),
                   jax.ShapeDtypeStruct((batch, n_heads, HEAD_DIM, HEAD_DIM), F32)),
        grid=(batch, n_heads),
        in_specs=[pl.BlockSpec(memory_space=pltpu.SMEM),
                  head_spec(0), head_spec(1), head_spec(2), head_spec(3),
                  table_spec, table_spec,
                  pl.BlockSpec((None, 1, HEAD_DIM), lambda b, h: (h, 0, 0))],
        out_specs=(pl.BlockSpec((seq, HEAD_DIM), lambda b, h: (b, h)),
                   pl.BlockSpec((None, None, HEAD_DIM, HEAD_DIM), lambda b, h: (b, h, 0, 0))),
        compiler_params=_params(2),
        name="ret_prompt",
    )(log_g, proj, proj, proj, proj, cos, sin, gain.reshape(n_heads, 1, HEAD_DIM))


def _sb_decode_tokens(k3, v3, qt, eye, bias, run, k_pos0, q_pos, scale):
    n_tok, n_heads, _ = k3.shape
    k2 = k3.reshape(n_tok * n_heads, HEAD_DIM).astype(BF16)
    v2 = v3.reshape(n_tok * n_heads, HEAD_DIM).astype(BF16)
    zt = _dot(k2, qt).reshape(n_tok, n_heads, n_heads)
    z = jnp.sum(jnp.where(eye[None], zt, 0.0), axis=1) * scale + bias
    k_pos = k_pos0 + lax.broadcasted_iota(jnp.int32, (n_tok, n_heads), 0)
    valid = k_pos < q_pos
    ls, l1m = _log_sigmoids(z)
    l1m = jnp.where(valid, l1m, 0.0)
    if n_tok > 1:
        r = lax.broadcasted_iota(jnp.int32, (n_tok, n_tok), 0)
        c = lax.broadcasted_iota(jnp.int32, (n_tok, n_tok), 1)
        later_tok = jnp.where(c > r, 1.0, 0.0).astype(BF16)
        hi = l1m.astype(BF16)
        lo = (l1m - hi.astype(F32)).astype(BF16)
        later = _dot(later_tok, hi) + _dot(later_tok, lo) + run
    else:
        later = run
    w = jnp.where(valid, jnp.exp(ls + later), 0.0)
    a = jnp.where(eye[None], w[:, None, :], 0.0).astype(BF16)
    a2 = a.reshape(n_tok * n_heads, n_heads)
    o = lax.dot_general(a2, v2, (((0,), (0,)), ((), ())), preferred_element_type=F32)
    return o, run + jnp.sum(l1m, axis=0, keepdims=True)


def _sb_decode_pages(k3s, v3s, q_rep, bias_f, run_f, k_pos0s, q_pos, scale, n_heads):
    n_tok = k3s[0].shape[0]
    per_row = LANES // n_heads
    n_rows = n_tok // per_row
    hb_n = n_heads // SUBLANES
    shape3 = (per_row * hb_n, SUBLANES, LANES)
    g = lax.broadcasted_iota(jnp.int32, shape3, 0)
    i = lax.broadcasted_iota(jnp.int32, shape3, 1)
    c = lax.broadcasted_iota(jnp.int32, shape3, 2)
    keep = ((c // n_heads) == (g // hb_n)) & ((c % n_heads) == (g % hb_n) * SUBLANES + i)
    r = lax.broadcasted_iota(jnp.int32, (n_rows, LANES), 0)
    cl = lax.broadcasted_iota(jnp.int32, (n_rows, LANES), 1)
    a = lax.broadcasted_iota(jnp.int32, (LANES, LANES), 0)
    b = lax.broadcasted_iota(jnp.int32, (LANES, LANES), 1)
    same_head = (a % n_heads) == (b % n_heads)
    later_in_row = jnp.where(same_head & (a > b), 1.0, 0.0).astype(BF16)
    whole_row = jnp.where(same_head, 1.0, 0.0).astype(BF16)
    ra = lax.broadcasted_iota(jnp.int32, (n_rows, n_rows), 0)
    rb = lax.broadcasted_iota(jnp.int32, (n_rows, n_rows), 1)
    later_rows = jnp.where(rb > ra, 1.0, 0.0).astype(BF16)
    own = (lax.broadcasted_iota(jnp.int32, (n_heads, LANES), 0)
           == lax.broadcasted_iota(jnp.int32, (n_heads, LANES), 1) % n_heads)

    zs = []
    for k3 in k3s:
        zt = _dot(k3.reshape(n_tok * n_heads, HEAD_DIM).astype(BF16), q_rep)
        z4 = zt.reshape(n_rows, per_row * hb_n, SUBLANES, LANES)
        zs.append(jnp.sum(jnp.sum(jnp.where(keep[None], z4, 0.0), axis=1), axis=1) * scale + bias_f)
    parts = []
    for z, k_pos0 in zip(zs, k_pos0s):
        valid = (k_pos0 + r * per_row + cl // n_heads) < q_pos
        ls, l1m = _log_sigmoids(z)
        l1m = jnp.where(valid, l1m, 0.0)
        row_tot = _split_dot(l1m, whole_row)
        within = _split_dot(l1m, later_in_row) + _split_dot_left(later_rows, row_tot)
        parts.append((valid, ls + within, jnp.sum(row_tot, axis=0, keepdims=True)))
    o = None
    for (valid, expo, tot), v3 in zip(parts, v3s):
        w = jnp.where(valid, jnp.exp(expo + run_f), 0.0)
        run_f = run_f + tot
        pieces = [jnp.where(own, w[u:u + 1, :], 0.0).astype(BF16) for u in range(n_rows)]
        at = jnp.concatenate(pieces, axis=1)
        ov = _dot(at, v3.reshape(n_tok * n_heads, HEAD_DIM).astype(BF16))
        o = ov if o is None else o + ov
    return o, run_f


def _sb_decode_kernel(pt_ref, q_ref, kn_ref, vn_ref, bias_ref, biasf_ref, gain_ref, *rest,
                      n_pages, n_heads, scale, group):
    del pt_ref
    kp_refs, vp_refs = rest[:group], rest[group:2 * group]
    o_ref, qrep_sc, run_sc, acc_sc = rest[2 * group:]
    p = pl.program_id(1)
    q_pos = n_pages * PAGE_SIZE
    reps = LANES // n_heads

    @pl.when(p == 0)
    def _():
        qt = q_ref[...].T.astype(BF16)
        qrep_sc[...] = jnp.concatenate([qt] * reps, axis=1)
        eye = (lax.broadcasted_iota(jnp.int32, (n_heads, n_heads), 0)
               == lax.broadcasted_iota(jnp.int32, (n_heads, n_heads), 1))
        o, run = _sb_decode_tokens(kn_ref[...][None], vn_ref[...][None], qt, eye, bias_ref[...],
                                   jnp.zeros((1, n_heads), F32), q_pos, q_pos, scale)
        acc_sc[...] = o
        run_sc[...] = jnp.concatenate([run] * reps, axis=1)

    first = n_pages - 1 - p * group
    o, run = _sb_decode_pages([ref[...] for ref in kp_refs], [ref[...] for ref in vp_refs],
                              qrep_sc[...], biasf_ref[...], run_sc[...],
                              [(first - j) * PAGE_SIZE for j in range(group)], q_pos, scale,
                              n_heads)
    acc_sc[...] += o
    run_sc[...] = run

    @pl.when(p == n_pages // group - 1)
    def _():
        o_ref[...] = _rms(acc_sc[...], gain_ref[...])


def _sb_decode(page_table, q, k_new, v_new, bias, gain, cache_k, cache_v, layer, group=4):
    n_seq, n_pages = page_table.shape
    n_heads = q.shape[1]
    assert n_pages % group == 0 and LANES % n_heads == 0 and n_heads % SUBLANES == 0
    kernel = functools.partial(_sb_decode_kernel, n_pages=n_pages, n_heads=n_heads,
                               scale=HEAD_DIM ** -0.5, group=group)
    tok_spec = pl.BlockSpec((None, n_heads, HEAD_DIM), lambda b, p, pt: (b, 0, 0))

    def page_spec(j):
        return pl.BlockSpec((None, None, PAGE_SIZE, n_heads, HEAD_DIM),
                            lambda b, p, pt: (layer, pt[b, n_pages - 1 - (p * group + j)], 0, 0, 0))

    pages = [page_spec(j) for j in range(group)]
    bias_f = jnp.tile(bias, (1, LANES // n_heads))
    return pl.pallas_call(
        kernel,
        out_shape=jax.ShapeDtypeStruct((n_seq, n_heads, HEAD_DIM), F32),
        grid_spec=pltpu.PrefetchScalarGridSpec(
            num_scalar_prefetch=1,
            grid=(n_seq, n_pages // group),
            in_specs=[tok_spec, tok_spec, tok_spec,
                      pl.BlockSpec((1, n_heads), lambda b, p, pt: (0, 0)),
                      pl.BlockSpec((1, LANES), lambda b, p, pt: (0, 0)),
                      pl.BlockSpec((n_heads, HEAD_DIM), lambda b, p, pt: (0, 0))] + pages + pages,
            out_specs=tok_spec,
            scratch_shapes=[pltpu.VMEM((HEAD_DIM, LANES), BF16),
                            pltpu.VMEM((1, LANES), F32),
                            pltpu.VMEM((n_heads, HEAD_DIM), F32)],
        ),
        compiler_params=_params(2),
        name="sb_decode",
    )(page_table, q, k_new, v_new, bias, bias_f, gain, *([cache_k] * group), *([cache_v] * group))


def _ret_decode_kernel(q_ref, k_ref, v_ref, g_ref, cos_ref, sin_ref, gain_ref, logg_ref, st_ref,
                       o_ref, sto_ref, *, n_heads, scale):
    c = cos_ref[...]
    s = sin_ref[...]
    qf = _rope(q_ref[...], c, s)
    kf = _rope(k_ref[...], c, s) * scale
    v = v_ref[...]
    gamma = jnp.exp(logg_ref[...])
    qt = qf.T
    kt = kf.T
    score = jnp.sum(qf * kf, axis=-1, keepdims=True)
    rows = []
    for h in range(n_heads):
        st = st_ref[h]
        g_h = gamma[h:h + 1, :]
        v_h = v[h:h + 1, :]
        read = jnp.sum(qt[:, h:h + 1] * st, axis=0, keepdims=True)
        rows.append(score[h:h + 1, :] * v_h + read * g_h)
        sto_ref[h] = st * g_h + kt[:, h:h + 1] * v_h
    o = jnp.concatenate(rows, axis=0)
    gate = g_ref[...]
    o_ref[...] = gate * jax.nn.sigmoid(gate) * _group_norm(o, gain_ref[...])


def _ret_decode(q, k, v, g, cos, sin, gain, log_g, state, layer):
    n_seq, n_heads, _ = q.shape
    kernel = functools.partial(_ret_decode_kernel, n_heads=n_heads, scale=HEAD_DIM ** -0.5)
    head_spec = pl.BlockSpec((None, n_heads, HEAD_DIM), lambda b: (b, 0, 0))
    return pl.pallas_call(
        kernel,
        out_shape=(jax.ShapeDtypeStruct((n_seq, n_heads, HEAD_DIM), F32),
                   jax.ShapeDtypeStruct((n_seq, n_heads, HEAD_DIM, HEAD_DIM), F32)),
        grid=(n_seq,),
        in_specs=[head_spec, head_spec, head_spec, head_spec,
                  pl.BlockSpec((1, HEAD_DIM), lambda b: (0, 0)),
                  pl.BlockSpec((1, HEAD_DIM), lambda b: (0, 0)),
                  pl.BlockSpec((n_heads, HEAD_DIM), lambda b: (0, 0)),
                  pl.BlockSpec((n_heads, 1), lambda b: (0, 0)),
                  pl.BlockSpec((None, None, n_heads, HEAD_DIM, HEAD_DIM),
                               lambda b: (layer, b, 0, 0, 0))],
        out_specs=(head_spec,
                   pl.BlockSpec((None, n_heads, HEAD_DIM, HEAD_DIM), lambda b: (b, 0, 0, 0))),
        compiler_params=_params(1),
        name="ret_decode",
    )(q, k, v, g, cos, sin, gain, log_g, state)


def kernel(x_prompt, x_sample, cache_k, cache_v, state_ret, page_table, w_in, w_out, sb_gain,
           gn_gain, sb_bias, ffn1_gate, ffn1_up, ffn1_down, ffn2_gate, ffn2_up, ffn2_down,
           norm_ffn1_pre, norm_ffn1_post, norm_mix_pre, norm_mix_post, norm_ffn2_pre,
           norm_ffn2_post):
    batch, seq, d_model = x_prompt.shape
    n_seq, dec_seq, _ = x_sample.shape
    depth = w_in.shape[0]
    n_sb = sb_gain.shape[1]
    n_ret = gn_gain.shape[1]
    sb_w = n_sb * HEAD_DIM
    ret_w = n_ret * HEAD_DIM
    m = batch * seq
    ms = n_seq * dec_seq
    past_len = page_table.shape[1] * PAGE_SIZE

    w_out_b = _cast_bf16(w_out)
    ffn_w = ((ffn1_gate, ffn1_up, _cast_bf16(ffn1_down)), (ffn2_gate, ffn2_up, _cast_bf16(ffn2_down)))
    ffn_post = (norm_ffn1_post, norm_ffn2_post)

    log_g = jnp.log1p(-(2.0 ** (-5.0 - jnp.arange(n_ret, dtype=F32))))
    cos_p, sin_p = _rope_tables(seq, 0, 1)
    cos_s, sin_s = _rope_tables(8, past_len, 0)

    xp = x_prompt.reshape(m, d_model)
    xs = x_sample.reshape(ms, d_model)
    rows_p, rows_s = 256, ms
    hp = _norm_cast(xp, norm_ffn1_pre[0], rows_p)
    hs = _norm_cast(xs, norm_ffn1_pre[0], rows_s)

    def half_ffn(xp, hp, xs, hs, which, layer, g_next):
        wg, wu, wd = ffn_w[which]
        g_post = ffn_post[which][layer]
        f, wg_b, wu_b = _ffn_up_ws(hp, wg, wu, layer, 1024, 256)
        y = _mm(f, wd, layer, 512, 512, F32)
        xp, hp = _resid_norm(xp, y, g_post, g_next, 0.5, rows_p)
        fs = _ffn_up(hs, wg_b, wu_b, None, ms, 512)
        ys = _mm(fs, wd, layer, ms, 512, F32)
        xs, hs = _resid_norm(xs, ys, g_post, g_next, 0.5, rows_s)
        return xp, hp, xs, hs

    kp_l, vp_l, stp_l, ks_l, vs_l, sts_l = [], [], [], [], [], []
    for l in range(depth):
        xp, hp, xs, hs = half_ffn(xp, hp, xs, hs, 0, l, norm_mix_pre[l])

        proj, w_in_b = _mm_ws(hp, w_in, l, 1024, 512, F32)
        a_sb = _sb_prompt(proj, sb_bias[l], sb_gain[l], batch, seq, n_sb)
        a_r, st_p = _ret_prompt(proj, log_g, cos_p, sin_p, gn_gain[l], batch, seq, n_ret,
                                3 * n_sb)
        y = _mm_out(a_sb, a_r, w_out_b, l, 1024, 512)
        xp, hp = _resid_norm(xp, y, norm_mix_post[l], norm_ffn2_pre[l], 1.0, rows_p)
        kp_l.append(proj[:, sb_w:2 * sb_w].reshape(batch, seq, n_sb, HEAD_DIM))
        vp_l.append(proj[:, 2 * sb_w:3 * sb_w].reshape(batch, seq, n_sb, HEAD_DIM))
        stp_l.append(st_p)

        proj_s = _mm(hs, w_in_b, None, ms, 512, F32)
        q_s = proj_s[:, :sb_w].reshape(ms, n_sb, HEAD_DIM)
        k_s = proj_s[:, sb_w:2 * sb_w].reshape(ms, n_sb, HEAD_DIM)
        v_s = proj_s[:, 2 * sb_w:3 * sb_w].reshape(ms, n_sb, HEAD_DIM)
        a_sb_s = _sb_decode(page_table, q_s, k_s, v_s, sb_bias[l].reshape(1, n_sb),
                            sb_gain[l], cache_k, cache_v, l)
        r0 = 3 * sb_w
        parts = [proj_s[:, r0 + i * ret_w:r0 + (i + 1) * ret_w].reshape(ms, n_ret, HEAD_DIM)
                 for i in range(4)]
        a_r_s, st_s = _ret_decode(parts[0], parts[1], parts[2], parts[3], cos_s[:1], sin_s[:1],
                                  gn_gain[l], log_g.reshape(n_ret, 1), state_ret, l)
        y = _mm_out(a_sb_s.reshape(ms, sb_w).astype(BF16), a_r_s.reshape(ms, ret_w).astype(BF16),
                    w_out_b, l, ms, 512)
        xs, hs = _resid_norm(xs, y, norm_mix_post[l], norm_ffn2_pre[l], 1.0, rows_s)
        ks_l.append(k_s.reshape(n_seq, dec_seq, n_sb, HEAD_DIM))
        vs_l.append(v_s.reshape(n_seq, dec_seq, n_sb, HEAD_DIM))
        sts_l.append(st_s)

        g_next = norm_ffn1_pre[l + 1] if l + 1 < depth else None
        xp, hp, xs, hs = half_ffn(xp, hp, xs, hs, 1, l, g_next)

    return (xp.reshape(batch, seq, d_model), xs.reshape(n_seq, dec_seq, d_model),
            jnp.stack(kp_l), jnp.stack(vp_l), jnp.stack(stp_l),
            jnp.stack(ks_l), jnp.stack(vs_l), jnp.stack(sts_l))
```

```python
import functools

import jax
import jax.numpy as jnp
from jax import lax
from jax.experimental import pallas as pl
from jax.experimental.pallas import tpu as pltpu

F32 = jnp.float32
BF16 = jnp.bfloat16

HEAD_DIM = 128
ROPE_BASE = 10000.0
NORM_EPS = 1e-6
GN_EPS = 1e-5
RET_CHUNK = 128
PAGE_SIZE = 128

V7X_VMEM_LIMIT_BYTES = 56 * 1024 * 1024
LANES = 128


def _params(n_axes):
    return pltpu.CompilerParams(
        dimension_semantics=("arbitrary",) * n_axes,
        vmem_limit_bytes=V7X_VMEM_LIMIT_BYTES,
    )


def _dot(a, b):
    return jnp.dot(a, b, preferred_element_type=F32)


def _dot_nt(a, b):
    return lax.dot_general(a, b, (((1,), (1,)), ((), ())), preferred_element_type=F32)


def _cast_kernel(w_ref, o_ref):
    o_ref[...] = w_ref[...].astype(BF16)


def _cast_bf16(w, rows=256):
    depth, k, n = w.shape
    return pl.pallas_call(
        _cast_kernel,
        out_shape=jax.ShapeDtypeStruct(w.shape, BF16),
        grid=(depth, k // rows),
        in_specs=[pl.BlockSpec((None, rows, n), lambda d, i: (d, i, 0))],
        out_specs=pl.BlockSpec((None, rows, n), lambda d, i: (d, i, 0)),
        compiler_params=_params(2),
        name="cast_bf16",
    )(w)


def _rms(x, g):
    return x * lax.rsqrt(jnp.mean(x * x, axis=-1, keepdims=True) + NORM_EPS) * g


def _norm_cast_kernel(x_ref, g_ref, h_ref):
    h_ref[...] = _rms(x_ref[...], g_ref[...]).astype(BF16)


def _norm_cast(x, g, rows):
    m, d = x.shape
    return pl.pallas_call(
        _norm_cast_kernel,
        out_shape=jax.ShapeDtypeStruct((m, d), BF16),
        grid=(m // rows,),
        in_specs=[pl.BlockSpec((rows, d), lambda i: (i, 0)),
                  pl.BlockSpec((1, d), lambda i: (0, 0))],
        out_specs=pl.BlockSpec((rows, d), lambda i: (i, 0)),
        compiler_params=_params(1),
        name="norm_cast",
    )(x, g.reshape(1, d))


def _resid_norm_kernel(x_ref, y_ref, gp_ref, gn_ref, xo_ref, h_ref, *, scale):
    xn = x_ref[...] + scale * _rms(y_ref[...], gp_ref[...])
    xo_ref[...] = xn
    h_ref[...] = _rms(xn, gn_ref[...]).astype(BF16)


def _resid_kernel(x_ref, y_ref, gp_ref, xo_ref, *, scale):
    xo_ref[...] = x_ref[...] + scale * _rms(y_ref[...], gp_ref[...])


def _resid_norm(x, y, g_post, g_next, scale, rows):
    m, d = x.shape
    row_spec = pl.BlockSpec((rows, d), lambda i: (i, 0))
    g_spec = pl.BlockSpec((1, d), lambda i: (0, 0))
    if g_next is None:
        return pl.pallas_call(
            functools.partial(_resid_kernel, scale=scale),
            out_shape=jax.ShapeDtypeStruct((m, d), F32),
            grid=(m // rows,),
            in_specs=[row_spec, row_spec, g_spec],
            out_specs=row_spec,
            compiler_params=_params(1),
            name="resid",
        )(x, y, g_post.reshape(1, d)), None
    return pl.pallas_call(
        functools.partial(_resid_norm_kernel, scale=scale),
        out_shape=(jax.ShapeDtypeStruct((m, d), F32), jax.ShapeDtypeStruct((m, d), BF16)),
        grid=(m // rows,),
        in_specs=[row_spec, row_spec, g_spec, g_spec],
        out_specs=(row_spec, row_spec),
        compiler_params=_params(1),
        name="resid_norm",
    )(x, y, g_post.reshape(1, d), g_next.reshape(1, d))


def _mm_kernel(x_ref, *refs):
    *w_refs, o_ref = refs
    kk = x_ref.shape[1] // len(w_refs)
    acc = _dot(x_ref[:, :kk], w_refs[0][...])
    for s in range(1, len(w_refs)):
        acc += _dot(x_ref[:, s * kk:(s + 1) * kk], w_refs[s][...])
    o_ref[...] = acc.astype(o_ref.dtype)


def _split_dot_k(x_refs, wb_ref):
    kk = wb_ref.shape[0] // len(x_refs)
    acc = _dot(x_refs[0][...], wb_ref[:kk, :])
    for s in range(1, len(x_refs)):
        acc += _dot(x_refs[s][...], wb_ref[s * kk:(s + 1) * kk, :])
    return acc


def _mm2_kernel(x1_ref, x2_ref, w1_ref, w2_ref, o_ref):
    acc = _dot(x1_ref[...], w1_ref[...]) + _dot(x2_ref[...], w2_ref[...])
    o_ref[...] = acc.astype(o_ref.dtype)


def _ffn_up_kernel(x_ref, wg_ref, wu_ref, o_ref):
    x = x_ref[...]
    g = _dot(x, wg_ref[...])
    u = _dot(x, wu_ref[...])
    o_ref[...] = (g * jax.nn.sigmoid(g) * u).astype(o_ref.dtype)


def _w_spec(w, layer, tn, slab=0, n_slabs=1):
    kk = w.shape[-2] // n_slabs
    if w.ndim == 2:
        return pl.BlockSpec((kk, tn), lambda i, j: (slab, j))
    return pl.BlockSpec((None, kk, tn), lambda i, j: (layer, slab, j))


def _mm(x, w, layer, tm, tn, out_dtype, k_slabs=1):
    m, k = x.shape
    n = w.shape[-1]
    assert k % (k_slabs * LANES) == 0
    return pl.pallas_call(
        _mm_kernel,
        out_shape=jax.ShapeDtypeStruct((m, n), out_dtype),
        grid=(m // tm, pl.cdiv(n, tn)),
        in_specs=[pl.BlockSpec((tm, k), lambda i, j: (i, 0))]
        + [_w_spec(w, layer, tn, s, k_slabs) for s in range(k_slabs)],
        out_specs=pl.BlockSpec((tm, tn), lambda i, j: (i, j)),
        compiler_params=_params(2),
        name="mm",
    )(x, *([w] * k_slabs))


def _mm_ws_kernel(*refs, k_slabs):
    x_refs = refs[:k_slabs]
    w_ref, o_ref, wb_ref = refs[k_slabs:]

    @pl.when(pl.program_id(1) == 0)
    def _():
        wb_ref[...] = w_ref[...].astype(BF16)

    o_ref[...] = _split_dot_k(x_refs, wb_ref).astype(o_ref.dtype)


def _ffn_up_ws_kernel(*refs, k_slabs):
    x_refs = refs[:k_slabs]
    wg_ref, wu_ref, o_ref, wgb_ref, wub_ref = refs[k_slabs:]

    @pl.when(pl.program_id(1) == 0)
    def _():
        wgb_ref[...] = wg_ref[...].astype(BF16)
        wub_ref[...] = wu_ref[...].astype(BF16)

    g = _split_dot_k(x_refs, wgb_ref)
    u = _split_dot_k(x_refs, wub_ref)
    o_ref[...] = (g * jax.nn.sigmoid(g) * u).astype(o_ref.dtype)


def _x_slab_specs(tm, k, k_slabs):
    assert k % (k_slabs * LANES) == 0
    return [pl.BlockSpec((tm, k // k_slabs), lambda j, i, s=s: (i, s)) for s in range(k_slabs)]


def _mm_ws(x, w, layer, tm, tn, out_dtype, k_slabs=2):
    m, k = x.shape
    n = w.shape[2]
    assert n % tn == 0 and m % tm == 0
    return pl.pallas_call(
        functools.partial(_mm_ws_kernel, k_slabs=k_slabs),
        out_shape=(jax.ShapeDtypeStruct((m, n), out_dtype), jax.ShapeDtypeStruct((k, n), BF16)),
        grid=(n // tn, m // tm),
        in_specs=_x_slab_specs(tm, k, k_slabs)
        + [pl.BlockSpec((None, k, tn), lambda j, i: (layer, 0, j))],
        out_specs=(pl.BlockSpec((tm, tn), lambda j, i: (i, j)),
                   pl.BlockSpec((k, tn), lambda j, i: (0, j))),
        compiler_params=_params(2),
        name="mm_ws",
    )(*([x] * k_slabs), w)


def _ffn_up_ws(x, wg, wu, layer, tm, tn, k_slabs=2):
    m, k = x.shape
    n = wg.shape[2]
    assert n % tn == 0 and m % tm == 0
    w_spec = pl.BlockSpec((None, k, tn), lambda j, i: (layer, 0, j))
    wb_spec = pl.BlockSpec((k, tn), lambda j, i: (0, j))
    wb_shape = jax.ShapeDtypeStruct((k, n), BF16)
    return pl.pallas_call(
        functools.partial(_ffn_up_ws_kernel, k_slabs=k_slabs),
        out_shape=(jax.ShapeDtypeStruct((m, n), BF16), wb_shape, wb_shape),
        grid=(n // tn, m // tm),
        in_specs=_x_slab_specs(tm, k, k_slabs) + [w_spec, w_spec],
        out_specs=(pl.BlockSpec((tm, tn), lambda j, i: (i, j)), wb_spec, wb_spec),
        compiler_params=_params(2),
        name="ffn_up_ws",
    )(*([x] * k_slabs), wg, wu)


def _mm_out(a_sb, a_r, w, layer, tm, tn):
    m, k = a_sb.shape
    n = w.shape[2]
    return pl.pallas_call(
        _mm2_kernel,
        out_shape=jax.ShapeDtypeStruct((m, n), F32),
        grid=(m // tm, n // tn),
        in_specs=[pl.BlockSpec((tm, k), lambda i, j: (i, 0)),
                  pl.BlockSpec((tm, k), lambda i, j: (i, 0)),
                  pl.BlockSpec((None, k, tn), lambda i, j: (layer, 0, j)),
                  pl.BlockSpec((None, k, tn), lambda i, j: (layer, 1, j))],
        out_specs=pl.BlockSpec((tm, tn), lambda i, j: (i, j)),
        compiler_params=_params(2),
        name="mm_out",
    )(a_sb, a_r, w, w)


def _ffn_up(x, wg, wu, layer, tm, tn):
    m, k = x.shape
    n = wg.shape[-1]
    w_spec = _w_spec(wg, layer, tn)
    return pl.pallas_call(
        _ffn_up_kernel,
        out_shape=jax.ShapeDtypeStruct((m, n), BF16),
        grid=(m // tm, pl.cdiv(n, tn)),
        in_specs=[pl.BlockSpec((tm, k), lambda i, j: (i, 0)), w_spec, w_spec],
        out_specs=pl.BlockSpec((tm, tn), lambda i, j: (i, j)),
        compiler_params=_params(2),
        name="ffn_up",
    )(x, wg, wu)


def _rope_table_kernel(inv_ref, cos_ref, sin_ref, *, pos0, stride):
    shape = cos_ref.shape
    row = lax.broadcasted_iota(jnp.int32, shape, 0)
    lane = lax.broadcasted_iota(jnp.int32, shape, 1)
    pos = (pos0 + stride * row).astype(F32)
    ang = pos * inv_ref[...]
    cos_ref[...] = jnp.cos(ang)
    s = jnp.sin(ang)
    sin_ref[...] = jnp.where(lane < HEAD_DIM // 2, -s, s)


def _rope_tables(n_rows, pos0, stride):
    half = HEAD_DIM // 2
    inv = ROPE_BASE ** (-jnp.arange(half, dtype=F32) / half)
    inv = jnp.concatenate([inv, inv]).reshape(1, HEAD_DIM)
    out = jax.ShapeDtypeStruct((n_rows, HEAD_DIM), F32)
    return pl.pallas_call(
        functools.partial(_rope_table_kernel, pos0=pos0, stride=stride),
        out_shape=(out, out),
        name="rope_tables",
    )(inv)


def _rope(x, cos, sin_signed):
    return x * cos + pltpu.roll(x, HEAD_DIM // 2, 1) * sin_signed


LOG2E = 1.4426950408889634
SUBLANES = 8


def _log_sigmoids(z):
    t = jnp.log(1.0 + jnp.exp(-jnp.abs(z)))
    return jnp.minimum(z, 0.0) - t, -jnp.maximum(z, 0.0) - t


def _split_dot(a, ones_bf16):
    hi = a.astype(BF16)
    lo = (a - hi.astype(F32)).astype(BF16)
    return _dot(hi, ones_bf16) + _dot(lo, ones_bf16)


def _split_dot_left(ones_bf16, a):
    hi = a.astype(BF16)
    lo = (a - hi.astype(F32)).astype(BF16)
    return _dot(ones_bf16, hi) + _dot(ones_bf16, lo)


def _sb_prompt_kernel(bias_ref, q_ref, k_ref, v_ref, gain_ref, o_ref, kbf, vbf, acc_sc, zn_sc,
                      w_sc, *, tq, nh, scale):
    hg = pl.program_id(1)
    qi = pl.program_id(2)

    @pl.when(qi == 0)
    def _():
        kbf[...] = k_ref[...].astype(BF16)
        vbf[...] = v_ref[...].astype(BF16)

    r = lax.broadcasted_iota(jnp.int32, (LANES, LANES), 0)
    c = lax.broadcasted_iota(jnp.int32, (LANES, LANES), 1)
    this_and_later = jnp.where(r >= c, 1.0, 0.0).astype(BF16)
    n_sub = tq // LANES
    row = lax.broadcasted_iota(jnp.int32, (tq, LANES), 0)
    col = lax.broadcasted_iota(jnp.int32, (tq, LANES), 1)
    qs = [q_ref[:, h * HEAD_DIM:(h + 1) * HEAD_DIM].astype(BF16) for h in range(nh)]
    nbias = [-bias_ref[hg * nh + h] for h in range(nh)]

    def neg_scores(k0):
        return [_dot_nt(qs[h], kbf[pl.ds(k0, tq), h * HEAD_DIM:(h + 1) * HEAD_DIM]) * (-scale)
                + nbias[h] for h in range(nh)]

    def weights_times_values(k0):
        for h in range(nh):
            hs = slice(h * HEAD_DIM, (h + 1) * HEAD_DIM)
            acc_sc[:, hs] += _dot(w_sc[:, h * tq:(h + 1) * tq], vbf[pl.ds(k0, tq), hs])

    def block(zns, runs, masked, k_next):
        tiles = [(h, s) for h in range(nh) for s in reversed(range(n_sub))]
        valid = {s: (s * LANES + col) < row for s in range(n_sub)} if masked else None
        expo = {}
        runs = list(runs)
        for (h, s) in tiles:
            zn = zns[h][:, s * LANES:(s + 1) * LANES]
            l = jnp.minimum(zn, 0.0) - jnp.log(1.0 + jnp.exp2(jnp.abs(zn) * (-LOG2E)))
            if masked:
                l = jnp.where(valid[s], l, 0.0)
            expo[h, s] = _split_dot(l, this_and_later) + (runs[h] - zn)
            runs[h] = runs[h] + jnp.sum(l, axis=-1, keepdims=True)
        nxt = neg_scores(k_next)
        for h in range(nh):
            zn_sc[:, h * tq:(h + 1) * tq] = nxt[h]
        for (h, s) in tiles:
            w = jnp.exp2(expo[h, s] * LOG2E)
            if masked:
                w = jnp.where(valid[s], w, 0.0)
            w_sc[:, h * tq + s * LANES:h * tq + (s + 1) * LANES] = w.astype(BF16)
        return tuple(runs)

    def start(j):
        return pl.multiple_of(jnp.maximum(j, 0) * tq, tq)

    acc_sc[...] = jnp.zeros_like(acc_sc)
    runs = block(neg_scores(start(qi)), tuple(jnp.zeros((tq, 1), F32) for _ in range(nh)), True,
                 start(qi - 1))

    def body(t, runs):
        zns = [zn_sc[:, h * tq:(h + 1) * tq] for h in range(nh)]
        weights_times_values(start(qi - t))
        return block(zns, runs, False, start(qi - 2 - t))

    lax.fori_loop(0, qi, body, runs)
    weights_times_values(0)
    for h in range(nh):
        hs = slice(h * HEAD_DIM, (h + 1) * HEAD_DIM)
        o_ref[:, hs] = _rms(acc_sc[:, hs], gain_ref[:, hs]).astype(o_ref.dtype)


def _sb_prompt(proj, bias, gain, batch, seq, n_heads, tq=256, nh=4):
    nq = seq // tq
    ng = n_heads // nh
    w = nh * HEAD_DIM
    kernel = functools.partial(_sb_prompt_kernel, tq=tq, nh=nh, scale=HEAD_DIM ** -0.5)
    return pl.pallas_call(
        kernel,
        out_shape=jax.ShapeDtypeStruct((batch * seq, n_heads * HEAD_DIM), BF16),
        grid=(batch, ng, nq),
        in_specs=[
            pl.BlockSpec(memory_space=pltpu.SMEM),
            pl.BlockSpec((tq, w), lambda b, g, i: (b * nq + i, g)),
            pl.BlockSpec((seq, w), lambda b, g, i: (b, ng + g)),
            pl.BlockSpec((seq, w), lambda b, g, i: (b, 2 * ng + g)),
            pl.BlockSpec((1, w), lambda b, g, i: (0, g)),
        ],
        out_specs=pl.BlockSpec((tq, w), lambda b, g, i: (b * nq + i, g)),
        scratch_shapes=[pltpu.VMEM((seq, w), BF16), pltpu.VMEM((seq, w), BF16),
                        pltpu.VMEM((tq, w), F32), pltpu.VMEM((tq, nh * tq), F32),
                        pltpu.VMEM((tq, nh * tq), BF16)],
        compiler_params=_params(3),
        name="sb_prompt",
    )(bias, proj, proj, proj, gain.reshape(1, n_heads * HEAD_DIM))


def _group_norm(o, g):
    mu = jnp.mean(o, axis=-1, keepdims=True)
    d = o - mu
    var = jnp.mean(d * d, axis=-1, keepdims=True)
    return d * lax.rsqrt(var + GN_EPS) * g


def _ret_prompt_kernel(logg_ref, q_ref, k_ref, v_ref, g_ref, cos_ref, sin_ref, gain_ref,
                       o_ref, st_ref, *, chunk, n_chunks, scale):
    h = pl.program_id(1)
    lg = logg_ref[h]
    ri = lax.broadcasted_iota(jnp.int32, (chunk, chunk), 0)
    ci = lax.broadcasted_iota(jnp.int32, (chunk, chunk), 1)
    diff = (ri - ci).astype(F32)
    decay = jnp.where(diff >= 0, jnp.exp(lg * jnp.maximum(diff, 0.0)), 0.0)
    idx = lax.broadcasted_iota(jnp.int32, (chunk, 1), 0).astype(F32)
    q_dec = jnp.exp(lg * (idx + 1.0))
    k_dec = jnp.exp(lg * (chunk - 1.0 - idx))
    s_dec = jnp.exp(jnp.full((1, 1), lg * chunk, F32))
    gain = gain_ref[...]

    state = jnp.zeros((HEAD_DIM, HEAD_DIM), F32)
    for n in range(n_chunks):
        rows = slice(n * chunk, (n + 1) * chunk)
        c = cos_ref[rows, :]
        s = sin_ref[rows, :]
        qf = _rope(q_ref[rows, :], c, s)
        kf = _rope(k_ref[rows, :], c, s) * scale
        qb = qf.astype(BF16)
        vb = v_ref[rows, :].astype(BF16)
        scores = _dot_nt(qb, kf.astype(BF16)) * decay
        o = _dot(scores.astype(BF16), vb) + _dot(qb, state.astype(BF16)) * q_dec
        kd_t = (kf * k_dec).T.astype(BF16)
        state = state * s_dec + _dot(kd_t, vb)
        gate = g_ref[rows, :]
        o_ref[rows, :] = (gate * jax.nn.sigmoid(gate) * _group_norm(o, gain)).astype(o_ref.dtype)
    st_ref[...] = state


def _ret_prompt(proj, log_g, cos, sin, gain, batch, seq, n_heads, col0):
    n_chunks = seq // RET_CHUNK
    kernel = functools.partial(_ret_prompt_kernel, chunk=RET_CHUNK, n_chunks=n_chunks,
                               scale=HEAD_DIM ** -0.5)

    def head_spec(group):
        return pl.BlockSpec((seq, HEAD_DIM), lambda b, h: (b, col0 + group * n_heads + h))

    table_spec = pl.BlockSpec((seq, HEAD_DIM), lambda b, h: (0, 0))
    return pl.pallas_call(
        kernel,
        out_shape=(jax.ShapeDtypeStruct((batch * seq, n_heads * HEAD_DIM), BF16),
                   jax.ShapeDtypeStruct((batch, n_heads, HEAD_DIM, HEAD_DIM), F32)),
        grid=(batch, n_heads),
        in_specs=[pl.BlockSpec(memory_space=pltpu.SMEM),
                  head_spec(0), head_spec(1), head_spec(2), head_spec(3),
                  table_spec, table_spec,
                  pl.BlockSpec((None, 1, HEAD_DIM), lambda b, h: (h, 0, 0))],
        out_specs=(pl.BlockSpec((seq, HEAD_DIM), lambda b, h: (b, h)),
                   pl.BlockSpec((None, None, HEAD_DIM, HEAD_DIM), lambda b, h: (b, h, 0, 0))),
        compiler_params=_params(2),
        name="ret_prompt",
    )(log_g, proj, proj, proj, proj, cos, sin, gain.reshape(n_heads, 1, HEAD_DIM))


def _sb_decode_tokens(k3, v3, qt, eye, bias, run, k_pos0, q_pos, scale):
    n_tok, n_heads, _ = k3.shape
    k2 = k3.reshape(n_tok * n_heads, HEAD_DIM).astype(BF16)
    v2 = v3.reshape(n_tok * n_heads, HEAD_DIM).astype(BF16)
    zt = _dot(k2, qt).reshape(n_tok, n_heads, n_heads)
    z = jnp.sum(jnp.where(eye[None], zt, 0.0), axis=1) * scale + bias
    k_pos = k_pos0 + lax.broadcasted_iota(jnp.int32, (n_tok, n_heads), 0)
    valid = k_pos < q_pos
    ls, l1m = _log_sigmoids(z)
    l1m = jnp.where(valid, l1m, 0.0)
    if n_tok > 1:
        r = lax.broadcasted_iota(jnp.int32, (n_tok, n_tok), 0)
        c = lax.broadcasted_iota(jnp.int32, (n_tok, n_tok), 1)
        later_tok = jnp.where(c > r, 1.0, 0.0).astype(BF16)
        hi = l1m.astype(BF16)
        lo = (l1m - hi.astype(F32)).astype(BF16)
        later = _dot(later_tok, hi) + _dot(later_tok, lo) + run
    else:
        later = run
    w = jnp.where(valid, jnp.exp(ls + later), 0.0)
    a = jnp.where(eye[None], w[:, None, :], 0.0).astype(BF16)
    a2 = a.reshape(n_tok * n_heads, n_heads)
    o = lax.dot_general(a2, v2, (((0,), (0,)), ((), ())), preferred_element_type=F32)
    return o, run + jnp.sum(l1m, axis=0, keepdims=True)


def _sb_decode_pages(k3s, v3s, q_rep, bias_f, run_f, k_pos0s, q_pos, scale, n_heads):
    n_tok = k3s[0].shape[0]
    per_row = LANES // n_heads
    n_rows = n_tok // per_row
    hb_n = n_heads // SUBLANES
    shape3 = (per_row * hb_n, SUBLANES, LANES)
    g = lax.broadcasted_iota(jnp.int32, shape3, 0)
    i = lax.broadcasted_iota(jnp.int32, shape3, 1)
    c = lax.broadcasted_iota(jnp.int32, shape3, 2)
    keep = ((c // n_heads) == (g // hb_n)) & ((c % n_heads) == (g % hb_n) * SUBLANES + i)
    r = lax.broadcasted_iota(jnp.int32, (n_rows, LANES), 0)
    cl = lax.broadcasted_iota(jnp.int32, (n_rows, LANES), 1)
    a = lax.broadcasted_iota(jnp.int32, (LANES, LANES), 0)
    b = lax.broadcasted_iota(jnp.int32, (LANES, LANES), 1)
    same_head = (a % n_heads) == (b % n_heads)
    later_in_row = jnp.where(same_head & (a > b), 1.0, 0.0).astype(BF16)
    whole_row = jnp.where(same_head, 1.0, 0.0).astype(BF16)
    ra = lax.broadcasted_iota(jnp.int32, (n_rows, n_rows), 0)
    rb = lax.broadcasted_iota(jnp.int32, (n_rows, n_rows), 1)
    later_rows = jnp.where(rb > ra, 1.0, 0.0).astype(BF16)
    own = (lax.broadcasted_iota(jnp.int32, (n_heads, LANES), 0)
           == lax.broadcasted_iota(jnp.int32, (n_heads, LANES), 1) % n_heads)

    zs = []
    for k3 in k3s:
        zt = _dot(k3.reshape(n_tok * n_heads, HEAD_DIM).astype(BF16), q_rep)
        z4 = zt.reshape(n_rows, per_row * hb_n, SUBLANES, LANES)
        zs.append(jnp.sum(jnp.sum(jnp.where(keep[None], z4, 0.0), axis=1), axis=1) * scale + bias_f)
    parts = []
    for z, k_pos0 in zip(zs, k_pos0s):
        valid = (k_pos0 + r * per_row + cl // n_heads) < q_pos
        ls, l1m = _log_sigmoids(z)
        l1m = jnp.where(valid, l1m, 0.0)
        row_tot = _split_dot(l1m, whole_row)
        within = _split_dot(l1m, later_in_row) + _split_dot_left(later_rows, row_tot)
        parts.append((valid, ls + within, jnp.sum(row_tot, axis=0, keepdims=True)))
    o = None
    for (valid, expo, tot), v3 in zip(parts, v3s):
        w = jnp.where(valid, jnp.exp(expo + run_f), 0.0)
        run_f = run_f + tot
        pieces = [jnp.where(own, w[u:u + 1, :], 0.0).astype(BF16) for u in range(n_rows)]
        at = jnp.concatenate(pieces, axis=1)
        ov = _dot(at, v3.reshape(n_tok * n_heads, HEAD_DIM).astype(BF16))
        o = ov if o is None else o + ov
    return o, run_f


def _sb_decode_kernel(pt_ref, q_ref, kn_ref, vn_ref, bias_ref, biasf_ref, gain_ref, *rest,
                      n_pages, n_heads, scale, group):
    del pt_ref
    kp_refs, vp_refs = rest[:group], rest[group:2 * group]
    o_ref, qrep_sc, run_sc, acc_sc = rest[2 * group:]
    p = pl.program_id(1)
    q_pos = n_pages * PAGE_SIZE
    reps = LANES // n_heads

    @pl.when(p == 0)
    def _():
        qt = q_ref[...].T.astype(BF16)
        qrep_sc[...] = jnp.concatenate([qt] * reps, axis=1)
        eye = (lax.broadcasted_iota(jnp.int32, (n_heads, n_heads), 0)
               == lax.broadcasted_iota(jnp.int32, (n_heads, n_heads), 1))
        o, run = _sb_decode_tokens(kn_ref[...][None], vn_ref[...][None], qt, eye, bias_ref[...],
                                   jnp.zeros((1, n_heads), F32), q_pos, q_pos, scale)
        acc_sc[...] = o
        run_sc[...] = jnp.concatenate([run] * reps, axis=1)

    first = n_pages - 1 - p * group
    o, run = _sb_decode_pages([ref[...] for ref in kp_refs], [ref[...] for ref in vp_refs],
                              qrep_sc[...], biasf_ref[...], run_sc[...],
                              [(first - j) * PAGE_SIZE for j in range(group)], q_pos, scale,
                              n_heads)
    acc_sc[...] += o
    run_sc[...] = run

    @pl.when(p == n_pages // group - 1)
    def _():
        o_ref[...] = _rms(acc_sc[...], gain_ref[...])


def _sb_decode(page_table, q, k_new, v_new, bias, gain, cache_k, cache_v, layer, group=4):
    n_seq, n_pages = page_table.shape
    n_heads = q.shape[1]
    assert n_pages % group == 0 and LANES % n_heads == 0 and n_heads % SUBLANES == 0
    kernel = functools.partial(_sb_decode_kernel, n_pages=n_pages, n_heads=n_heads,
                               scale=HEAD_DIM ** -0.5, group=group)
    tok_spec = pl.BlockSpec((None, n_heads, HEAD_DIM), lambda b, p, pt: (b, 0, 0))

    def page_spec(j):
        return pl.BlockSpec((None, None, PAGE_SIZE, n_heads, HEAD_DIM),
                            lambda b, p, pt: (layer, pt[b, n_pages - 1 - (p * group + j)], 0, 0, 0))

    pages = [page_spec(j) for j in range(group)]
    bias_f = jnp.tile(bias, (1, LANES // n_heads))
    return pl.pallas_call(
        kernel,
        out_shape=jax.ShapeDtypeStruct((n_seq, n_heads, HEAD_DIM), F32),
        grid_spec=pltpu.PrefetchScalarGridSpec(
            num_scalar_prefetch=1,
            grid=(n_seq, n_pages // group),
            in_specs=[tok_spec, tok_spec, tok_spec,
                      pl.BlockSpec((1, n_heads), lambda b, p, pt: (0, 0)),
                      pl.BlockSpec((1, LANES), lambda b, p, pt: (0, 0)),
                      pl.BlockSpec((n_heads, HEAD_DIM), lambda b, p, pt: (0, 0))] + pages + pages,
            out_specs=tok_spec,
            scratch_shapes=[pltpu.VMEM((HEAD_DIM, LANES), BF16),
                            pltpu.VMEM((1, LANES), F32),
                            pltpu.VMEM((n_heads, HEAD_DIM), F32)],
        ),
        compiler_params=_params(2),
        name="sb_decode",
    )(page_table, q, k_new, v_new, bias, bias_f, gain, *([cache_k] * group), *([cache_v] * group))


def _ret_decode_kernel(q_ref, k_ref, v_ref, g_ref, cos_ref, sin_ref, gain_ref, logg_ref, st_ref,
                       o_ref, sto_ref, *, n_heads, scale):
    c = cos_ref[...]
    s = sin_ref[...]
    qf = _rope(q_ref[...], c, s)
    kf = _rope(k_ref[...], c, s) * scale
    v = v_ref[...]
    gamma = jnp.exp(logg_ref[...])
    qt = qf.T
    kt = kf.T
    score = jnp.sum(qf * kf, axis=-1, keepdims=True)
    rows = []
    for h in range(n_heads):
        st = st_ref[h]
        g_h = gamma[h:h + 1, :]
        v_h = v[h:h + 1, :]
        read = jnp.sum(qt[:, h:h + 1] * st, axis=0, keepdims=True)
        rows.append(score[h:h + 1, :] * v_h + read * g_h)
        sto_ref[h] = st * g_h + kt[:, h:h + 1] * v_h
    o = jnp.concatenate(rows, axis=0)
    gate = g_ref[...]
    o_ref[...] = gate * jax.nn.sigmoid(gate) * _group_norm(o, gain_ref[...])


def _ret_decode(q, k, v, g, cos, sin, gain, log_g, state, layer):
    n_seq, n_heads, _ = q.shape
    kernel = functools.partial(_ret_decode_kernel, n_heads=n_heads, scale=HEAD_DIM ** -0.5)
    head_spec = pl.BlockSpec((None, n_heads, HEAD_DIM), lambda b: (b, 0, 0))
    return pl.pallas_call(
        kernel,
        out_shape=(jax.ShapeDtypeStruct((n_seq, n_heads, HEAD_DIM), F32),
                   jax.ShapeDtypeStruct((n_seq, n_heads, HEAD_DIM, HEAD_DIM), F32)),
        grid=(n_seq,),
        in_specs=[head_spec, head_spec, head_spec, head_spec,
                  pl.BlockSpec((1, HEAD_DIM), lambda b: (0, 0)),
                  pl.BlockSpec((1, HEAD_DIM), lambda b: (0, 0)),
                  pl.BlockSpec((n_heads, HEAD_DIM), lambda b: (0, 0)),
                  pl.BlockSpec((n_heads, 1), lambda b: (0, 0)),
                  pl.BlockSpec((None, None, n_heads, HEAD_DIM, HEAD_DIM),
                               lambda b: (layer, b, 0, 0, 0))],
        out_specs=(head_spec,
                   pl.BlockSpec((None, n_heads, HEAD_DIM, HEAD_DIM), lambda b: (b, 0, 0, 0))),
        compiler_params=_params(1),
        name="ret_decode",
    )(q, k, v, g, cos, sin, gain, log_g, state)


def kernel(x_prompt, x_sample, cache_k, cache_v, state_ret, page_table, w_in, w_out, sb_gain,
           gn_gain, sb_bias, ffn1_gate, ffn1_up, ffn1_down, ffn2_gate, ffn2_up, ffn2_down,
           norm_ffn1_pre, norm_ffn1_post, norm_mix_pre, norm_mix_post, norm_ffn2_pre,
           norm_ffn2_post):
    batch, seq, d_model = x_prompt.shape
    n_seq, dec_seq, _ = x_sample.shape
    depth = w_in.shape[0]
    n_sb = sb_gain.shape[1]
    n_ret = gn_gain.shape[1]
    sb_w = n_sb * HEAD_DIM
    ret_w = n_ret * HEAD_DIM
    m = batch * seq
    ms = n_seq * dec_seq
    past_len = page_table.shape[1] * PAGE_SIZE

    w_out_b = _cast_bf16(w_out)
    ffn_w = ((ffn1_gate, ffn1_up, _cast_bf16(ffn1_down)), (ffn2_gate, ffn2_up, _cast_bf16(ffn2_down)))
    ffn_post = (norm_ffn1_post, norm_ffn2_post)

    log_g = jnp.log1p(-(2.0 ** (-5.0 - jnp.arange(n_ret, dtype=F32))))
    cos_p, sin_p = _rope_tables(seq, 0, 1)
    cos_s, sin_s = _rope_tables(8, past_len, 0)

    xp = x_prompt.reshape(m, d_model)
    xs = x_sample.reshape(ms, d_model)
    rows_p, rows_s = 256, ms
    hp = _norm_cast(xp, norm_ffn1_pre[0], rows_p)
    hs = _norm_cast(xs, norm_ffn1_pre[0], rows_s)

    def half_ffn(xp, hp, xs, hs, which, layer, g_next):
        wg, wu, wd = ffn_w[which]
        g_post = ffn_post[which][layer]
        f, wg_b, wu_b = _ffn_up_ws(hp, wg, wu, layer, 1024, 256)
        y = _mm(f, wd, layer, 512, 512, F32, k_slabs=2)
        xp, hp = _resid_norm(xp, y, g_post, g_next, 0.5, rows_p)
        fs = _ffn_up(hs, wg_b, wu_b, None, ms, 512)
        ys = _mm(fs, wd, layer, ms, 512, F32)
        xs, hs = _resid_norm(xs, ys, g_post, g_next, 0.5, rows_s)
        return xp, hp, xs, hs

    kp_l, vp_l, stp_l, ks_l, vs_l, sts_l = [], [], [], [], [], []
    for l in range(depth):
        xp, hp, xs, hs = half_ffn(xp, hp, xs, hs, 0, l, norm_mix_pre[l])

        proj, w_in_b = _mm_ws(hp, w_in, l, 1024, 512, F32)
        a_sb = _sb_prompt(proj, sb_bias[l], sb_gain[l], batch, seq, n_sb)
        a_r, st_p = _ret_prompt(proj, log_g, cos_p, sin_p, gn_gain[l], batch, seq, n_ret,
                                3 * n_sb)
        y = _mm_out(a_sb, a_r, w_out_b, l, 1024, 512)
        xp, hp = _resid_norm(xp, y, norm_mix_post[l], norm_ffn2_pre[l], 1.0, rows_p)
        kp_l.append(proj[:, sb_w:2 * sb_w].reshape(batch, seq, n_sb, HEAD_DIM))
        vp_l.append(proj[:, 2 * sb_w:3 * sb_w].reshape(batch, seq, n_sb, HEAD_DIM))
        stp_l.append(st_p)

        proj_s = _mm(hs, w_in_b, None, ms, 512, F32)
        q_s = proj_s[:, :sb_w].reshape(ms, n_sb, HEAD_DIM)
        k_s = proj_s[:, sb_w:2 * sb_w].reshape(ms, n_sb, HEAD_DIM)
        v_s = proj_s[:, 2 * sb_w:3 * sb_w].reshape(ms, n_sb, HEAD_DIM)
        a_sb_s = _sb_decode(page_table, q_s, k_s, v_s, sb_bias[l].reshape(1, n_sb),
                            sb_gain[l], cache_k, cache_v, l)
        r0 = 3 * sb_w
        parts = [proj_s[:, r0 + i * ret_w:r0 + (i + 1) * ret_w].reshape(ms, n_ret, HEAD_DIM)
                 for i in range(4)]
        a_r_s, st_s = _ret_decode(parts[0], parts[1], parts[2], parts[3], cos_s[:1], sin_s[:1],
                                  gn_gain[l], log_g.reshape(n_ret, 1), state_ret, l)
        y = _mm_out(a_sb_s.reshape(ms, sb_w).astype(BF16), a_r_s.reshape(ms, ret_w).astype(BF16),
                    w_out_b, l, ms, 512)
        xs, hs = _resid_norm(xs, y, norm_mix_post[l], norm_ffn2_pre[l], 1.0, rows_s)
        ks_l.append(k_s.reshape(n_seq, dec_seq, n_sb, HEAD_DIM))
        vs_l.append(v_s.reshape(n_seq, dec_seq, n_sb, HEAD_DIM))
        sts_l.append(st_s)

        g_next = norm_ffn1_pre[l + 1] if l + 1 < depth else None
        xp, hp, xs, hs = half_ffn(xp, hp, xs, hs, 1, l, g_next)

    return (xp.reshape(batch, seq, d_model), xs.reshape(n_seq, dec_seq, d_model),
            jnp.stack(kp_l), jnp.stack(vp_l), jnp.stack(stp_l),
            jnp.stack(ks_l), jnp.stack(vs_l), jnp.stack(sts_l))
```
